```python
import jax, jax.numpy as jnp
from jax import lax
import numpy as np

D_MODEL = 1024
BATCH = 16
SEQ = 2048
DEPTH = 2

GRID_W = 64
D_ATTN = D_MODEL // 2
D_HGRN = D_MODEL // 4
D_CONV = D_MODEL - D_ATTN - D_HGRN
HEAD_DIM = 64
N_HEADS = D_ATTN // HEAD_DIM
N_KV_HEADS = 2
KV_GROUP = N_HEADS // N_KV_HEADS
ROPE_THETA = 10000.0
Q_BLOCK = 128
HGRN_HEAD_DIM = 64
HGRN_HEADS = D_HGRN // HGRN_HEAD_DIM
HGRN_CHUNK = 64
F_MIN = 1e-6
CONV_WIDTH = 31
CONV_PAD = (CONV_WIDTH - 1) // 2
D_FF = ((8 * D_MODEL // 3 + 255) // 256) * 256
EPS = 1e-6
LN_EPS = 1e-5
SPLITS = (D_ATTN, N_KV_HEADS * HEAD_DIM, N_KV_HEADS * HEAD_DIM,
          D_HGRN, D_HGRN, D_HGRN, D_HGRN, D_HGRN, D_CONV, D_CONV)
D_IN_PROJ = int(sum(SPLITS))
SPLIT_IDX = tuple(int(c) for c in np.cumsum(SPLITS)[:-1])

kernel_name = "hybrid_attn_hgrn2_conformer_encoder"


def _rms_norm(x, w, eps=EPS):
    xf = x.astype(jnp.float32)
    y = xf * lax.rsqrt(jnp.mean(xf * xf, axis=-1, keepdims=True) + eps)
    return (y * w.astype(jnp.float32)).astype(x.dtype)


def _layer_norm(x, w, b, eps=LN_EPS):
    xf = x.astype(jnp.float32)
    mu = jnp.mean(xf, axis=-1, keepdims=True)
    xc = xf - mu
    var = jnp.mean(xc * xc, axis=-1, keepdims=True)
    y = xc * lax.rsqrt(var + eps) * w.astype(jnp.float32) + b.astype(jnp.float32)
    return y.astype(x.dtype)


def _axial_rope_tables(seq_len, dtype):
    rows = seq_len // GRID_W
    row_id = jnp.repeat(jnp.arange(rows, dtype=jnp.float32), GRID_W)
    col_id = jnp.tile(jnp.arange(GRID_W, dtype=jnp.float32), rows)
    half = HEAD_DIM // 2
    inv_freq = ROPE_THETA ** (-jnp.arange(0, half, 2, dtype=jnp.float32) / half)
    ang_r = row_id[:, None] * inv_freq[None, :]
    ang_c = col_id[:, None] * inv_freq[None, :]
    ang = jnp.concatenate([ang_r, ang_r, ang_c, ang_c], axis=-1)
    return jnp.cos(ang).astype(dtype), jnp.sin(ang).astype(dtype)


def _axial_rope(x, cos, sin):
    x_r1, x_r2, x_c1, x_c2 = jnp.split(x, 4, axis=-1)
    rot = jnp.concatenate([-x_r2, x_r1, -x_c2, x_c1], axis=-1)
    return x * cos[None, :, None, :] + rot * sin[None, :, None, :]


def _gqa_attention(q, k, v, cos, sin):
    B, S = q.shape[0], q.shape[1]
    q = _axial_rope(q, cos, sin) * (HEAD_DIM ** -0.5)
    k = _axial_rope(k, cos, sin)
    qb = q.reshape(B, S // Q_BLOCK, Q_BLOCK, N_KV_HEADS, KV_GROUP, HEAD_DIM)
    qb = qb.transpose(1, 0, 2, 3, 4, 5)

    def one_block(q_blk):
        s = jnp.einsum('bqkgd,bskd->bkgqs', q_blk, k, preferred_element_type=jnp.float32)
        p = jax.nn.softmax(s, axis=-1).astype(v.dtype)
        return jnp.einsum('bkgqs,bskd->bqkgd', p, v)

    o = lax.map(one_block, qb)
    return o.transpose(1, 0, 2, 3, 4, 5).reshape(B, S, D_ATTN)


def _hgrn2_scan(q, k, v, log_f):
    B, S = q.shape[0], q.shape[1]
    n_chunks = S // HGRN_CHUNK

    def to_chunks(a):
        return a.reshape(B, n_chunks, HGRN_CHUNK, HGRN_HEADS, a.shape[-1]).transpose(1, 0, 3, 2, 4)

    in_chunk_mask = jnp.tril(jnp.ones((HGRN_CHUNK, HGRN_CHUNK), dtype=bool))
    mask5 = in_chunk_mask[None, None, :, :, None]

    def step(state, xs):
        qc, kc, vc, gc = xs
        b = jnp.cumsum(gc, axis=2)
        diff = b[:, :, :, None, :] - b[:, :, None, :, :]
        decay = jnp.where(mask5, jnp.exp(jnp.where(mask5, diff, 0.0)), 0.0)
        scores = jnp.einsum('bhtd,bhsd,bhtsd->bhts', qc, kc, decay)
        o = (jnp.einsum('bhts,bhse->bhte', scores, vc)
             + jnp.einsum('bhtd,bhde->bhte', qc * jnp.exp(b), state))
        b_last = b[:, :, -1:, :]
        new_state = (jnp.exp(b_last[:, :, 0, :])[..., None] * state
                     + jnp.einsum('bhsd,bhse->bhde', kc * jnp.exp(b_last - b), vc))
        return new_state, o

    state0 = jnp.zeros((B, HGRN_HEADS, q.shape[-1], v.shape[-1]), jnp.float32)
    _, o = lax.scan(step, state0, (to_chunks(q), to_chunks(k), to_chunks(v), to_chunks(log_f)))
    return o.transpose(1, 0, 3, 2, 4).reshape(B, S, HGRN_HEADS, v.shape[-1])


def _hgrn2_gates(z, lb):
    z = z.astype(jnp.float32)
    lb = lb.reshape(1, 1, HGRN_HEADS, HGRN_HEAD_DIM)
    f = lb + (1.0 - lb) * jax.nn.sigmoid(z)
    log_f = jnp.log(jnp.maximum(f, F_MIN))
    k = (1.0 - lb) * jax.nn.sigmoid(-z)
    return log_f, k


def _hgrn2_mixer(hq, hf_fwd, hf_bwd, hi, hg, lb_fwd, lb_bwd, gnorm_w):
    B, S = hq.shape[0], hq.shape[1]
    shp = (B, S, HGRN_HEADS, HGRN_HEAD_DIM)
    q = hq.reshape(shp).astype(jnp.float32)
    v = hi.reshape(shp).astype(jnp.float32)
    log_f_fw, k_fw = _hgrn2_gates(hf_fwd.reshape(shp), lb_fwd)
    log_f_bw, k_bw = _hgrn2_gates(hf_bwd.reshape(shp), lb_bwd)
    o_fw = _hgrn2_scan(q, k_fw, v, log_f_fw)
    o_bw = _hgrn2_scan(q[:, ::-1], k_bw[:, ::-1], v[:, ::-1], log_f_bw[:, ::-1])[:, ::-1]
    o = (o_fw + o_bw).astype(hq.dtype)
    g = hg.reshape(shp)
    o = _rms_norm(o, gnorm_w) * jax.nn.silu(g)
    return o.reshape(B, S, D_HGRN)


def _conformer_conv(a, b, dw_w, dw_b, ln_w, ln_b, pw_w, pw_b):
    u = a * jax.nn.sigmoid(b)
    u = lax.conv_general_dilated(
        u, dw_w[:, None, :].astype(u.dtype), window_strides=(1,),
        padding=[(CONV_PAD, CONV_PAD)], dimension_numbers=('NWC', 'WIO', 'NWC'),
        feature_group_count=D_CONV) + dw_b
    u = jax.nn.silu(_layer_norm(u, ln_w, ln_b))
    return u @ pw_w + pw_b


def setup_inputs(seed: int = 0) -> dict:
    key = jax.random.key(seed)
    ks = jax.random.split(key, 24)
    f32 = jnp.float32

    def gain(k, shape):
        return 1.0 + 0.02 * jax.random.normal(k, shape, f32)

    def bias(k, shape):
        return 0.02 * jax.random.normal(k, shape, f32)

    def dense(k, shape, fan_in):
        return jax.random.normal(k, shape, f32) * fan_in ** -0.5

    return {
        "x": jax.random.normal(ks[0], (BATCH, SEQ, D_MODEL), f32),
        "mix_norm_w": gain(ks[1], (DEPTH, D_MODEL)),
        "w_in": dense(ks[2], (DEPTH, D_MODEL, D_IN_PROJ), D_MODEL),
        "q_norm_w": gain(ks[3], (DEPTH, HEAD_DIM)),
        "k_norm_w": gain(ks[4], (DEPTH, HEAD_DIM)),
        "hgrn_lb_logits": 0.5 * jax.random.normal(ks[5], (DEPTH, 2, D_HGRN), f32),
        "hgrn_gnorm_w": gain(ks[6], (DEPTH, HGRN_HEAD_DIM)),
        "conv_dw_w": dense(ks[7], (DEPTH, CONV_WIDTH, D_CONV), CONV_WIDTH),
        "conv_dw_b": bias(ks[8], (DEPTH, D_CONV)),
        "conv_ln_w": gain(ks[9], (DEPTH, D_CONV)),
        "conv_ln_b": bias(ks[10], (DEPTH, D_CONV)),
        "conv_pw_w": dense(ks[11], (DEPTH, D_CONV, D_CONV), D_CONV),
        "conv_pw_b": bias(ks[12], (DEPTH, D_CONV)),
        "attn_out_norm_w": gain(ks[13], (DEPTH, D_ATTN)),
        "conv_out_norm_w": gain(ks[14], (DEPTH, D_CONV)),
        "w_out": dense(ks[15], (DEPTH, D_MODEL, D_MODEL), D_MODEL),
        "ffn_norm_w": gain(ks[16], (DEPTH, D_MODEL)),
        "w_gate": dense(ks[17], (DEPTH, D_MODEL, D_FF), D_MODEL),
        "w_up": dense(ks[18], (DEPTH, D_MODEL, D_FF), D_MODEL),
        "w_down": dense(ks[19], (DEPTH, D_FF, D_MODEL), D_FF),
    }


def reference(x, mix_norm_w, w_in, q_norm_w, k_norm_w, hgrn_lb_logits, hgrn_gnorm_w,
              conv_dw_w, conv_dw_b, conv_ln_w, conv_ln_b, conv_pw_w, conv_pw_b,
              attn_out_norm_w, conv_out_norm_w, w_out, ffn_norm_w, w_gate, w_up, w_down):
    B, S = x.shape[0], x.shape[1]
    cos, sin = _axial_rope_tables(S, x.dtype)
    sm = jax.nn.softmax(hgrn_lb_logits.astype(jnp.float32), axis=0)
    lower_bounds = jnp.cumsum(sm, axis=0) - sm[0:1]

    for l in range(DEPTH):
        h = _rms_norm(x, mix_norm_w[l])
        proj = h @ w_in[l]
        q, k, v, hq, hf_fw, hf_bw, hi, hg, ca, cb = jnp.split(proj, SPLIT_IDX, axis=-1)

        q = _rms_norm(q.reshape(B, S, N_HEADS, HEAD_DIM), q_norm_w[l])
        k = _rms_norm(k.reshape(B, S, N_KV_HEADS, HEAD_DIM), k_norm_w[l])
        v = v.reshape(B, S, N_KV_HEADS, HEAD_DIM)
        y_attn = _gqa_attention(q, k, v, cos, sin)

        y_hgrn = _hgrn2_mixer(hq, hf_fw, hf_bw, hi, hg,
                              lower_bounds[l, 0], lower_bounds[l, 1], hgrn_gnorm_w[l])

        y_conv = _conformer_conv(ca, cb, conv_dw_w[l], conv_dw_b[l], conv_ln_w[l],
                                 conv_ln_b[l], conv_pw_w[l], conv_pw_b[l])

        mixed = jnp.concatenate([_rms_norm(y_attn, attn_out_norm_w[l]),
                                 y_hgrn,
                                 _rms_norm(y_conv, conv_out_norm_w[l])], axis=-1)
        x = x + mixed @ w_out[l]

        h = _rms_norm(x, ffn_norm_w[l])
        x = x + (jax.nn.silu(h @ w_gate[l]) * (h @ w_up[l])) @ w_down[l]
    return x
```

```python
import functools

import numpy as np
import jax
import jax.numpy as jnp
from jax import lax
from jax.experimental import pallas as pl
from jax.experimental.pallas import tpu as pltpu

F32 = jnp.float32
BF16 = jnp.bfloat16

D_MODEL = 1024
DEPTH = 2
GRID_W = 64
D_ATTN = 512
D_HGRN = 256
D_CONV = 256
HEAD_DIM = 64
N_HEADS = D_ATTN // HEAD_DIM
N_KV_HEADS = 2
D_KV = N_KV_HEADS * HEAD_DIM
ROPE_THETA = 10000.0
HGRN_HEADS = D_HGRN // HEAD_DIM
F_MIN = 1e-6
CONV_WIDTH = 31
CONV_PAD = (CONV_WIDTH - 1) // 2
D_FF = 2816
EPS = 1e-6
LN_EPS = 1e-5
D_IN_PROJ = D_ATTN + 2 * D_KV + 5 * D_HGRN + 2 * D_CONV

OFF_Q = 0
OFF_K = OFF_Q + D_ATTN
OFF_V = OFF_K + D_KV
OFF_H = OFF_V + D_KV
OFF_C = OFF_H + 5 * D_HGRN

LANES = 128
SUBLANES = 8
VMEM_LIMIT_BYTES = 56 * 1024 * 1024

TM_PROJ = 512
TQ_ATTN = 256
TM_FFN = 512
FF_CHUNK = 256
HG_CHUNK = 64
HG_FINE = 8
HG_REF = 2
CONV_ROWS = 64
CONV_HALO = 16

assert -np.log(F_MIN) * (HG_FINE - 1 - HG_REF) < 80.0 and -np.log(F_MIN) * HG_REF < 80.0


def _params(n_grid_axes):
    return pltpu.CompilerParams(
        dimension_semantics=("arbitrary",) * n_grid_axes,
        vmem_limit_bytes=VMEM_LIMIT_BYTES)


def _full(shape):
    nd = len(shape)
    return pl.BlockSpec(shape, lambda *_: (0,) * nd)


def _split2(y):
    hi = y.astype(BF16)
    lo = (y - hi.astype(F32)).astype(BF16)
    return hi, lo


def _segment_mean_sq(y, seg2_ref, seg_len):
    hi, lo = _split2(y * y)
    cat = jnp.concatenate([hi, lo], axis=1)
    return jnp.dot(cat, seg2_ref[...], preferred_element_type=F32) * (1.0 / seg_len)


def _rope(xn, cos, sin_signed, first_mask):
    width = xn.shape[1]
    left = pltpu.roll(xn, width - HEAD_DIM // 4, 1)
    right = pltpu.roll(xn, HEAD_DIM // 4, 1)
    rot = jnp.where(first_mask, left, right)
    return xn * cos + rot * sin_signed


def _inproj_kernel(x_ref, nw_ref, w_ref, qnw_ref, knw_ref, cq_ref, sq_ref, ck_ref, sk_ref,
                   segq_ref, segk_ref,
                   q_out, kt_out, va_out, vb_out, hg_out, cv_out):
    x = x_ref[...]
    ms = jnp.mean(x * x, axis=-1, keepdims=True)
    h = (x * lax.rsqrt(ms + EPS) * nw_ref[...]).astype(BF16)

    q = jnp.dot(h, w_ref[:, OFF_Q:OFF_K], preferred_element_type=F32)
    qn = q * lax.rsqrt(_segment_mean_sq(q, segq_ref, HEAD_DIM) + EPS) * qnw_ref[...]
    lane_q = lax.broadcasted_iota(jnp.int32, (1, D_ATTN), 1)
    first_q = (lane_q & (HEAD_DIM // 2 - 1)) < HEAD_DIM // 4
    q_out[...] = _rope(qn, cq_ref[...], sq_ref[...], first_q).astype(BF16)

    k = jnp.dot(h, w_ref[:, OFF_K:OFF_V], preferred_element_type=F32)
    kn = k * lax.rsqrt(_segment_mean_sq(k, segk_ref, HEAD_DIM) + EPS) * knw_ref[...]
    lane_k = lax.broadcasted_iota(jnp.int32, (1, D_KV), 1)
    first_k = (lane_k & (HEAD_DIM // 2 - 1)) < HEAD_DIM // 4
    kr = _rope(kn, ck_ref[...], sk_ref[...], first_k)
    kt_out[...] = kr.T.astype(BF16)

    v = jnp.dot(h, w_ref[:, OFF_V:OFF_H], preferred_element_type=F32)
    va_out[...] = v.astype(BF16)
    vb_out[...] = pltpu.roll(v, HEAD_DIM, 1).astype(BF16)

    hg_out[...] = jnp.dot(h, w_ref[:, OFF_H:OFF_C], preferred_element_type=F32)
    cv_out[...] = jnp.dot(h, w_ref[:, OFF_C:D_IN_PROJ], preferred_element_type=F32)


def _inproj(x2, nw, w_in, qnw, knw, tabs, segq, segk, batch, seq):
    tokens = x2.shape[0]
    n_seq_tiles = seq // TM_PROJ
    cq, sq, ck, sk = tabs
    tab_q = pl.BlockSpec((TM_PROJ, D_ATTN), lambda i: (i % n_seq_tiles, 0))
    tab_k = pl.BlockSpec((TM_PROJ, D_KV), lambda i: (i % n_seq_tiles, 0))
    row = lambda w: pl.BlockSpec((TM_PROJ, w), lambda i: (i, 0))
    return pl.pallas_call(
        _inproj_kernel,
        grid=(tokens // TM_PROJ,),
        in_specs=[row(D_MODEL), _full((1, D_MODEL)), _full((D_MODEL, D_IN_PROJ)),
                  _full((1, D_ATTN)), _full((1, D_KV)), tab_q, tab_q, tab_k, tab_k,
                  _full((2 * D_ATTN, D_ATTN)), _full((2 * D_KV, D_KV))],
        out_specs=[row(D_ATTN),
                   pl.BlockSpec((None, D_KV, TM_PROJ), lambda i: (i // n_seq_tiles, 0, i % n_seq_tiles)),
                   row(D_KV), row(D_KV), row(5 * D_HGRN), row(2 * D_CONV)],
        out_shape=[jax.ShapeDtypeStruct((tokens, D_ATTN), BF16),
                   jax.ShapeDtypeStruct((batch, D_KV, seq), BF16),
                   jax.ShapeDtypeStruct((tokens, D_KV), BF16),
                   jax.ShapeDtypeStruct((tokens, D_KV), BF16),
                   jax.ShapeDtypeStruct((tokens, 5 * D_HGRN), F32),
                   jax.ShapeDtypeStruct((tokens, 2 * D_CONV), F32)],
        compiler_params=_params(1),
        name="inproj",
    )(x2, nw, w_in, qnw, knw, cq, sq, ck, sk, segq, segk)


def _attn_kernel(q_ref, kt_ref, va_ref, vb_ref, nw_ref, o_ref):
    seq = kt_ref.shape[1]
    lane = lax.broadcasted_iota(jnp.int32, (1, D_KV), 1)
    low = lane < HEAD_DIM
    va = va_ref[...]
    vb = vb_ref[...]
    zero_v = jnp.zeros_like(va)
    v_place = ((jnp.where(low, va, zero_v), jnp.where(low, zero_v, vb)),
               (jnp.where(low, vb, zero_v), jnp.where(low, zero_v, va)))
    zero_k = jnp.zeros((HEAD_DIM, seq), BF16)
    pairs = []
    for j in range(N_HEADS // 2):
        g = (2 * j) // (N_HEADS // N_KV_HEADS)
        kt = kt_ref[g * HEAD_DIM:(g + 1) * HEAD_DIM, :]
        qp = q_ref[:, j * LANES:(j + 1) * LANES]
        acc = None
        for e in range(2):
            kpad = jnp.concatenate([kt, zero_k] if e == 0 else [zero_k, kt], axis=0)
            s = jnp.dot(qp, kpad, preferred_element_type=F32)
            m = jnp.max(s, axis=-1, keepdims=True)
            p = jnp.exp(s - m)
            l = jnp.sum(p, axis=-1, keepdims=True)
            o = jnp.dot(p.astype(BF16), v_place[g][e], preferred_element_type=F32) / l
            acc = o if acc is None else acc + o
        pairs.append(acc)
    y = jnp.concatenate(pairs, axis=1)
    ms = jnp.mean(y * y, axis=-1, keepdims=True)
    o_ref[...] = (y * lax.rsqrt(ms + EPS) * nw_ref[...]).astype(BF16)


def _attention(q, kt, va, vb, nw, batch, seq):
    tokens = q.shape[0]
    nq = seq // TQ_ATTN
    return pl.pallas_call(
        _attn_kernel,
        grid=(batch, nq),
        in_specs=[pl.BlockSpec((TQ_ATTN, D_ATTN), lambda b, j: (b * nq + j, 0)),
                  pl.BlockSpec((None, D_KV, seq), lambda b, j: (b, 0, 0)),
                  pl.BlockSpec((seq, D_KV), lambda b, j: (b, 0)),
                  pl.BlockSpec((seq, D_KV), lambda b, j: (b, 0)),
                  pl.BlockSpec((1, D_ATTN), lambda b, j: (0, 0))],
        out_specs=pl.BlockSpec((TQ_ATTN, D_ATTN), lambda b, j: (b * nq + j, 0)),
        out_shape=jax.ShapeDtypeStruct((tokens, D_ATTN), BF16),
        compiler_params=_params(2),
        name="attn",
    )(q, kt, va, vb, nw)


HG_LEVELS = (32, 16, 8)
N_EXP_GROUPS = 2 + 2 * (len(HG_LEVELS) + 1)


def _hgrn_constants():
    n = HG_CHUNK
    mats = []
    idx = np.arange(n)
    cum = (idx[None, :] <= idx[:, None]).astype(np.float32)
    last = (idx[None, :] > idx[:, None]).astype(np.float32)
    mats += [cum, last]
    for half in HG_LEVELS:
        mq = np.zeros((n, n), np.float32)
        mk = np.zeros((n, n), np.float32)
        for p in range(n):
            if (p // half) % 2 == 1:
                p0 = (p // half) * half
                mq[p, p0 + 1:p + 1] = 1.0
            else:
                p0 = (p // half) * half + half
                mk[p, p + 1:p0 + 1] = 1.0
        mats += [mq, mk]
    mq = np.zeros((n, n), np.float32)
    mk = np.zeros((n, n), np.float32)
    for p in range(n):
        pr = (p // HG_FINE) * HG_FINE + HG_REF
        if p > pr:
            mq[p, pr + 1:p + 1] = 1.0
            mk[p, pr + 1:p + 1] = -1.0
        elif p < pr:
            mq[p, p + 1:pr + 1] = -1.0
            mk[p, p + 1:pr + 1] = 1.0
    mats += [mq, mk]
    fwd = np.concatenate(mats, axis=0)
    bwd = np.concatenate([m[::-1, ::-1] for m in mats], axis=0)
    m3 = np.stack([np.tile(fwd, (1, 3)), np.tile(bwd, (1, 3))])

    lev = np.full((n, n), len(HG_LEVELS) + 1, np.int32)
    for t in range(n):
        for s in range(t + 1):
            if t // HG_FINE == s // HG_FINE:
                lev[t, s] = len(HG_LEVELS)
            else:
                for li, half in enumerate(HG_LEVELS):
                    if t // (2 * half) == s // (2 * half) and t // half != s // half:
                        lev[t, s] = li
                        break
    lev2 = np.stack([np.tile(lev, (1, HGRN_HEADS)), np.tile(lev[::-1, ::-1], (1, HGRN_HEADS))])
    head = np.arange(D_HGRN) // HEAD_DIM
    headmask = (head[:, None] == head[None, :]).astype(np.float32)
    return m3, lev2, headmask


def _hgrn_kernel(hq_ref, zf_ref, zb_ref, hi_ref, gate_ref, logit_ref, gnw_ref, m3_ref, lev_ref,
                 hm_ref, hmb_ref, seg_ref, o_ref, ofw_ref, st_ref, *, layer):
    seq = hq_ref.shape[0]
    n_chunks = seq // HG_CHUNK
    n = HG_CHUNK

    lg = [logit_ref[i] for i in range(DEPTH)]
    mx = functools.reduce(jnp.maximum, lg)
    ex = [jnp.exp(v - mx) for v in lg]
    den = functools.reduce(lambda a, b: a + b, ex)
    sm = [e / den for e in ex]
    acc = sm[0]
    for i in range(1, layer + 1):
        acc = acc + sm[i]
    lb = acc - sm[0]

    def chunk(d, z_ref, lb_row, r0):
        hm_bf = hmb_ref[...]
        z = z_ref[pl.ds(r0, n), :]
        q = hq_ref[pl.ds(r0, n), :]
        v = hi_ref[pl.ds(r0, n), :]
        f = lb_row + (1.0 - lb_row) * jax.nn.sigmoid(z)
        g = jnp.log(jnp.maximum(f, F_MIN))
        kk = (1.0 - lb_row) * jax.nn.sigmoid(-z)
        g1 = g.astype(BF16)
        r1 = g - g1.astype(F32)
        g2 = r1.astype(BF16)
        g3 = (r1 - g2.astype(F32)).astype(BF16)
        gcat = jnp.concatenate([g1, g2, g3], axis=0)
        expo = jnp.dot(m3_ref[d], gcat, preferred_element_type=F32)
        ex_all = jnp.exp(expo)
        x_cum = ex_all[0:n]
        x_last = ex_all[n:2 * n]
        decay_tot = x_cum[n - 1:n] if d == 0 else x_cum[0:1]

        lev = lev_ref[d]
        a = jnp.zeros((n, D_HGRN), F32)
        for li in range(len(HG_LEVELS) + 1):
            base = (2 + 2 * li) * n
            qt = (q * ex_all[base:base + n]).astype(BF16)
            kt = (kk * ex_all[base + n:base + 2 * n]).astype(BF16)
            kbd = jnp.concatenate([kt] * HGRN_HEADS, axis=0) * hm_bf
            sc = lax.dot_general(qt, kbd, (((1,), (1,)), ((), ())), preferred_element_type=F32)
            a = jnp.where(lev == li, sc, a)
        vbd = jnp.concatenate([v.astype(BF16)] * HGRN_HEADS, axis=0) * hm_bf
        o = jnp.dot(a.astype(BF16), vbd, preferred_element_type=F32)

        st = st_ref[...]
        qb = (q * x_cum).astype(BF16)
        o = o + lax.dot_general(qb, st.astype(BF16), (((1,), (1,)), ((), ())), preferred_element_type=F32)

        kdec = (kk * x_last).astype(BF16)
        zpad = jnp.zeros((LANES - n, D_HGRN), F32)
        vt = jnp.concatenate([v, zpad], axis=0).T.astype(BF16)
        kpad = jnp.concatenate([kdec, jnp.zeros((LANES - n, D_HGRN), BF16)], axis=0)
        upd = jnp.dot(vt, kpad, preferred_element_type=F32)
        st_ref[...] = (st * decay_tot + upd) * hm_ref[...]
        return o

    st_ref[...] = jnp.zeros_like(st_ref)

    def fwd_body(ci, carry):
        r0 = pl.multiple_of(ci * n, n)
        ofw_ref[pl.ds(r0, n), :] = chunk(0, zf_ref, lb[0:1], r0)
        return carry

    lax.fori_loop(0, n_chunks, fwd_body, 0)

    st_ref[...] = jnp.zeros_like(st_ref)
    gnw = gnw_ref[...]

    def bwd_body(ci, carry):
        r0 = pl.multiple_of((n_chunks - 1 - ci) * n, n)
        o = chunk(1, zb_ref, lb[1:2], r0) + ofw_ref[pl.ds(r0, n), :]
        on = o * lax.rsqrt(_segment_mean_sq(o, seg_ref, HEAD_DIM) + EPS) * gnw
        gate = gate_ref[pl.ds(r0, n), :]
        o_ref[pl.ds(r0, n), :] = (on * (gate * jax.nn.sigmoid(gate))).astype(BF16)
        return carry

    lax.fori_loop(0, n_chunks, bwd_body, 0)


def _hgrn(hg, logits, gnw, consts, seg, layer, batch, seq):
    tokens = hg.shape[0]
    m3, lev, hm, hmb = consts
    col = lambda j: pl.BlockSpec((seq, D_HGRN), lambda b: (b, j))
    return pl.pallas_call(
        functools.partial(_hgrn_kernel, layer=layer),
        grid=(batch,),
        in_specs=[col(0), col(1), col(2), col(3), col(4),
                  _full(logits.shape), _full((1, D_HGRN)), _full(m3.shape), _full(lev.shape),
                  _full(hm.shape), _full(hmb.shape), _full(seg.shape)],
        out_specs=pl.BlockSpec((seq, D_HGRN), lambda b: (b, 0)),
        out_shape=jax.ShapeDtypeStruct((tokens, D_HGRN), BF16),
        scratch_shapes=[pltpu.VMEM((seq, D_HGRN), F32), pltpu.VMEM((D_HGRN, D_HGRN), F32)],
        compiler_params=_params(1),
        name="hgrn",
    )(hg, hg, hg, hg, hg, logits, gnw, m3, lev, hm, hmb, seg)


def _conv_kernel(a_ref, b_ref, dww_ref, dwb_ref, lnw_ref, lnb_ref, pww_ref, pwb_ref, nw_ref,
                 o_ref, pad_ref):
    seq = a_ref.shape[0]
    halo = jnp.zeros((CONV_HALO, D_CONV), F32)
    pad_ref[0:CONV_HALO, :] = halo
    pad_ref[CONV_HALO + seq:CONV_HALO + seq + CONV_HALO, :] = halo
    pad_ref[CONV_HALO:CONV_HALO + seq, :] = a_ref[...] * jax.nn.sigmoid(b_ref[...])

    win_rows = CONV_ROWS + 2 * CONV_HALO

    def body(i, carry):
        r0 = pl.multiple_of(i * CONV_ROWS, CONV_ROWS)
        win = pad_ref[pl.ds(r0, win_rows), :]
        acc = jnp.zeros((CONV_ROWS, D_CONV), F32)
        for r in range(SUBLANES):
            shifted = win if r == 0 else pltpu.roll(win, win_rows - r, 0)
            for a8 in range(0, 2 * CONV_HALO, SUBLANES):
                tap = a8 + r - (CONV_HALO - CONV_PAD)
                if 0 <= tap < CONV_WIDTH:
                    acc = acc + shifted[a8:a8 + CONV_ROWS] * dww_ref[tap:tap + 1, :]
        u = acc + dwb_ref[...]
        mu = jnp.mean(u, axis=-1, keepdims=True)
        uc = u - mu
        var = jnp.mean(uc * uc, axis=-1, keepdims=True)
        y = uc * lax.rsqrt(var + LN_EPS) * lnw_ref[...] + lnb_ref[...]
        y = y * jax.nn.sigmoid(y)
        y = jnp.dot(y.astype(BF16), pww_ref[...], preferred_element_type=F32) + pwb_ref[...]
        ms = jnp.mean(y * y, axis=-1, keepdims=True)
        o_ref[pl.ds(r0, CONV_ROWS), :] = (y * lax.rsqrt(ms + EPS) * nw_ref[...]).astype(BF16)
        return carry

    lax.fori_loop(0, seq // CONV_ROWS, body, 0)


def _conv(cv, dww, dwb, lnw, lnb, pww, pwb, nw, batch, seq):
    tokens = cv.shape[0]
    col = lambda j: pl.BlockSpec((seq, D_CONV), lambda b: (b, j))
    vec = _full((1, D_CONV))
    return pl.pallas_call(
        _conv_kernel,
        grid=(batch,),
        in_specs=[col(0), col(1), _full(dww.shape), vec, vec, vec, _full((D_CONV, D_CONV)), vec, vec],
        out_specs=pl.BlockSpec((seq, D_CONV), lambda b: (b, 0)),
        out_shape=jax.ShapeDtypeStruct((tokens, D_CONV), BF16),
        scratch_shapes=[pltpu.VMEM((seq + 2 * CONV_HALO, D_CONV), F32)],
        compiler_params=_params(1),
        name="conv",
    )(cv, cv, dww, dwb, lnw, lnb, pww, pwb, nw)


def _ffn_kernel(x_ref, ya_ref, yh_ref, yc_ref, wo_ref, nw_ref, wg_ref, wu_ref, wd_ref,
                o_ref, act_ref):
    x1 = x_ref[...]
    x1 = x1 + jnp.dot(ya_ref[...], wo_ref[0:D_ATTN, :], preferred_element_type=F32)
    x1 = x1 + jnp.dot(yh_ref[...], wo_ref[D_ATTN:D_ATTN + D_HGRN, :], preferred_element_type=F32)
    x1 = x1 + jnp.dot(yc_ref[...], wo_ref[D_ATTN + D_HGRN:D_MODEL, :], preferred_element_type=F32)
    ms = jnp.mean(x1 * x1, axis=-1, keepdims=True)
    h = (x1 * lax.rsqrt(ms + EPS) * nw_ref[...]).astype(BF16)
    for c in range(0, D_FF, FF_CHUNK):
        g = jnp.dot(h, wg_ref[:, c:c + FF_CHUNK], preferred_element_type=F32)
        u = jnp.dot(h, wu_ref[:, c:c + FF_CHUNK], preferred_element_type=F32)
        act_ref[:, c:c + FF_CHUNK] = (g * jax.nn.sigmoid(g) * u).astype(BF16)
    o_ref[...] = x1 + jnp.dot(act_ref[...], wd_ref[...], preferred_element_type=F32)


def _ffn(x2, ya, yh, yc, wo, nw, wg, wu, wd):
    tokens = x2.shape[0]
    row = lambda w: pl.BlockSpec((TM_FFN, w), lambda i: (i, 0))
    resident = lambda shape: pl.BlockSpec(shape, lambda i: (0, 0), pipeline_mode=pl.Buffered(1))
    return pl.pallas_call(
        _ffn_kernel,
        grid=(tokens // TM_FFN,),
        in_specs=[row(D_MODEL), row(D_ATTN), row(D_HGRN), row(D_CONV),
                  resident((D_MODEL, D_MODEL)), _full((1, D_MODEL)),
                  resident((D_MODEL, D_FF)), resident((D_MODEL, D_FF)), resident((D_FF, D_MODEL))],
        out_specs=row(D_MODEL),
        out_shape=jax.ShapeDtypeStruct((tokens, D_MODEL), F32),
        scratch_shapes=[pltpu.VMEM((TM_FFN, D_FF), BF16)],
        compiler_params=_params(1),
        name="ffn",
    )(x2, ya, yh, yc, wo, nw, wg, wu, wd)


def _rope_tables(seq):
    rows = seq // GRID_W
    row_id = jnp.repeat(jnp.arange(rows, dtype=F32), GRID_W)
    col_id = jnp.tile(jnp.arange(GRID_W, dtype=F32), rows)
    half = HEAD_DIM // 2
    inv_freq = ROPE_THETA ** (-jnp.arange(0, half, 2, dtype=F32) / half)
    ang_r = row_id[:, None] * inv_freq[None, :]
    ang_c = col_id[:, None] * inv_freq[None, :]
    ang = jnp.concatenate([ang_r, ang_r, ang_c, ang_c], axis=-1)
    cos, sin = jnp.cos(ang), jnp.sin(ang)
    lane = np.arange(HEAD_DIM)
    sign = jnp.asarray(np.where((lane % (HEAD_DIM // 2)) < HEAD_DIM // 4, -1.0, 1.0), F32)
    sin_signed = sin * sign[None, :]
    scale = HEAD_DIM ** -0.5
    return (jnp.tile(cos, (1, N_HEADS)) * scale, jnp.tile(sin_signed, (1, N_HEADS)) * scale,
            jnp.tile(cos, (1, N_KV_HEADS)), jnp.tile(sin_signed, (1, N_KV_HEADS)))


def _segment_matrix2(width, seg_len):
    seg = np.arange(width) // seg_len
    m = (seg[:, None] == seg[None, :]).astype(np.float32)
    return jnp.asarray(np.concatenate([m, m], axis=0), BF16)


def kernel(x, mix_norm_w, w_in, q_norm_w, k_norm_w, hgrn_lb_logits, hgrn_gnorm_w, conv_dw_w, conv_dw_b,
           conv_ln_w, conv_ln_b, conv_pw_w, conv_pw_b, attn_out_norm_w, conv_out_norm_w, w_out,
           ffn_norm_w, w_gate, w_up, w_down):
    batch, seq, d_model = x.shape
    assert d_model == D_MODEL and seq % TM_PROJ == 0 and seq % TQ_ATTN == 0 and seq % HG_CHUNK == 0
    assert (batch * seq) % TM_FFN == 0 and w_in.shape == (DEPTH, D_MODEL, D_IN_PROJ)
    assert w_gate.shape == (DEPTH, D_MODEL, D_FF)

    tabs = _rope_tables(seq)
    segq = _segment_matrix2(D_ATTN, HEAD_DIM)
    segk = _segment_matrix2(D_KV, HEAD_DIM)
    segh = _segment_matrix2(D_HGRN, HEAD_DIM)
    m3, lev, hm = _hgrn_constants()
    hconsts = (jnp.asarray(m3, BF16), jnp.asarray(lev), jnp.asarray(hm), jnp.asarray(hm, BF16))
    logits = hgrn_lb_logits.astype(F32)

    x2 = x.reshape(batch * seq, D_MODEL)
    for l in range(DEPTH):
        q, kt, va, vb, hg, cv = _inproj(
            x2, mix_norm_w[l][None, :], w_in[l].astype(BF16),
            jnp.tile(q_norm_w[l], N_HEADS)[None, :], jnp.tile(k_norm_w[l], N_KV_HEADS)[None, :],
            tabs, segq, segk, batch, seq)
        ya = _attention(q, kt, va, vb, attn_out_norm_w[l][None, :], batch, seq)
        yh = _hgrn(hg, logits, jnp.tile(hgrn_gnorm_w[l], HGRN_HEADS)[None, :], hconsts, segh, l, batch, seq)
        yc = _conv(cv, conv_dw_w[l], conv_dw_b[l][None, :], conv_ln_w[l][None, :], conv_ln_b[l][None, :],
                   conv_pw_w[l].astype(BF16), conv_pw_b[l][None, :], conv_out_norm_w[l][None, :], batch, seq)
        x2 = _ffn(x2, ya, yh, yc, w_out[l].astype(BF16), ffn_norm_w[l][None, :],
                  w_gate[l].astype(BF16), w_up[l].astype(BF16), w_down[l].astype(BF16))
    return x2.reshape(batch, seq, D_MODEL)
```

```python
import functools

import numpy as np
import jax
import jax.numpy as jnp
from jax import lax
from jax.experimental import pallas as pl
from jax.experimental.pallas import tpu as pltpu

F32 = jnp.float32
BF16 = jnp.bfloat16

D_MODEL = 1024
DEPTH = 2
GRID_W = 64
D_ATTN = 512
D_HGRN = 256
D_CONV = 256
HEAD_DIM = 64
N_HEADS = D_ATTN // HEAD_DIM
N_KV_HEADS = 2
D_KV = N_KV_HEADS * HEAD_DIM
ROPE_THETA = 10000.0
HGRN_HEADS = D_HGRN // HEAD_DIM
F_MIN = 1e-6
CONV_WIDTH = 31
CONV_PAD = (CONV_WIDTH - 1) // 2
D_FF = 2816
EPS = 1e-6
LN_EPS = 1e-5
D_IN_PROJ = D_ATTN + 2 * D_KV + 5 * D_HGRN + 2 * D_CONV

OFF_Q = 0
OFF_K = OFF_Q + D_ATTN
OFF_V = OFF_K + D_KV
OFF_H = OFF_V + D_KV
OFF_C = OFF_H + 5 * D_HGRN

LANES = 128
SUBLANES = 8
VMEM_LIMIT_BYTES = 56 * 1024 * 1024

TM_PROJ = 512
TQ_ATTN = 256
ATTN_SLOTS = 3
TM_FFN = 512
FF_CHUNK = 256
HG_CHUNK = 64
HG_FINE = 8
HG_REF = 2
HG_FINISH_ROWS = 256
CONV_ROWS = 64
CONV_POST_ROWS = 256
CONV_HALO = 16

assert -np.log(F_MIN) * (HG_FINE - 1 - HG_REF) < 80.0 and -np.log(F_MIN) * HG_REF < 80.0


def _params(n_grid_axes):
    return pltpu.CompilerParams(
        dimension_semantics=("arbitrary",) * n_grid_axes,
        vmem_limit_bytes=VMEM_LIMIT_BYTES)


def _full(shape):
    nd = len(shape)
    return pl.BlockSpec(shape, lambda *_: (0,) * nd)


def _segment_mean_sq(y, seg_ref, seg_len):
    return jnp.dot((y * y).astype(BF16), seg_ref[...], preferred_element_type=F32) * (1.0 / seg_len)


def _rope(xn, cos, sin_signed, first_mask):
    width = xn.shape[1]
    left = pltpu.roll(xn, width - HEAD_DIM // 4, 1)
    right = pltpu.roll(xn, HEAD_DIM // 4, 1)
    rot = jnp.where(first_mask, left, right)
    return xn * cos + rot * sin_signed


def _inproj_kernel(x_ref, nw_ref, w_ref, qnw_ref, knw_ref, cq_ref, sq_ref, ck_ref, sk_ref,
                   segq_ref, segk_ref,
                   q_out, kp_out, va_out, hg_out, cv_out):
    x = x_ref[...]
    ms = jnp.mean(x * x, axis=-1, keepdims=True)
    h = (x * lax.rsqrt(ms + EPS) * nw_ref[...]).astype(BF16)

    q = jnp.dot(h, w_ref[:, OFF_Q:OFF_K], preferred_element_type=F32)
    qn = q * lax.rsqrt(_segment_mean_sq(q, segq_ref, HEAD_DIM) + EPS) * qnw_ref[...]
    lane_q = lax.broadcasted_iota(jnp.int32, (1, D_ATTN), 1)
    first_q = (lane_q & (HEAD_DIM // 2 - 1)) < HEAD_DIM // 4
    q_out[...] = _rope(qn, cq_ref[...], sq_ref[...], first_q).astype(BF16)

    k = jnp.dot(h, w_ref[:, OFF_K:OFF_V], preferred_element_type=F32)
    kn = k * lax.rsqrt(_segment_mean_sq(k, segk_ref, HEAD_DIM) + EPS) * knw_ref[...]
    lane_k = lax.broadcasted_iota(jnp.int32, (1, D_KV), 1)
    first_k = (lane_k & (HEAD_DIM // 2 - 1)) < HEAD_DIM // 4
    kr = _rope(kn, ck_ref[...], sk_ref[...], first_k)
    ks = pltpu.roll(kr, HEAD_DIM, 1)
    low = lane_k < HEAD_DIM
    zero = jnp.zeros_like(kr)
    kp_out[0] = jnp.where(low, kr, zero).astype(BF16)
    kp_out[1] = jnp.where(low, zero, ks).astype(BF16)
    kp_out[2] = jnp.where(low, ks, zero).astype(BF16)
    kp_out[3] = jnp.where(low, zero, kr).astype(BF16)

    v = jnp.dot(h, w_ref[:, OFF_V:OFF_H], preferred_element_type=F32)
    vt = v.T
    ones = jnp.ones((HEAD_DIM, vt.shape[1]), F32)
    va_out[0] = jnp.concatenate([vt[0:HEAD_DIM], ones], axis=0).astype(BF16)
    va_out[1] = jnp.concatenate([ones, vt[HEAD_DIM:D_KV]], axis=0).astype(BF16)

    hg_out[...] = jnp.dot(h, w_ref[:, OFF_H:OFF_C], preferred_element_type=F32)
    cv_out[...] = jnp.dot(h, w_ref[:, OFF_C:D_IN_PROJ], preferred_element_type=F32)


def _inproj(x2, nw, w_in, qnw, knw, tabs, segq, segk, batch, seq):
    tokens = x2.shape[0]
    n_seq_tiles = seq // TM_PROJ
    cq, sq, ck, sk = tabs
    tab_q = pl.BlockSpec((TM_PROJ, D_ATTN), lambda i: (i % n_seq_tiles, 0))
    tab_k = pl.BlockSpec((TM_PROJ, D_KV), lambda i: (i % n_seq_tiles, 0))
    row = lambda w: pl.BlockSpec((TM_PROJ, w), lambda i: (i, 0))
    return pl.pallas_call(
        _inproj_kernel,
        grid=(tokens // TM_PROJ,),
        in_specs=[row(D_MODEL), _full((1, D_MODEL)), _full((D_MODEL, D_IN_PROJ)),
                  _full((1, D_ATTN)), _full((1, D_KV)), tab_q, tab_q, tab_k, tab_k,
                  _full((D_ATTN, D_ATTN)), _full((D_KV, D_KV))],
        out_specs=[row(D_ATTN),
                   pl.BlockSpec((2 * N_KV_HEADS, TM_PROJ, D_KV), lambda i: (0, i, 0)),
                   pl.BlockSpec((None, N_KV_HEADS, D_KV, TM_PROJ),
                                lambda i: (i // n_seq_tiles, 0, 0, i % n_seq_tiles)),
                   row(5 * D_HGRN), row(2 * D_CONV)],
        out_shape=[jax.ShapeDtypeStruct((tokens, D_ATTN), BF16),
                   jax.ShapeDtypeStruct((2 * N_KV_HEADS, tokens, D_KV), BF16),
                   jax.ShapeDtypeStruct((batch, N_KV_HEADS, D_KV, seq), BF16),
                   jax.ShapeDtypeStruct((tokens, 5 * D_HGRN), F32),
                   jax.ShapeDtypeStruct((tokens, 2 * D_CONV), F32)],
        compiler_params=_params(1),
        name="inproj",
    )(x2, nw, w_in, qnw, knw, cq, sq, ck, sk, segq, segk)


def _attn_kernel(q_ref, kp_ref, va_ref, nw_ref, o_ref, st_ref):
    def scores_t(hd):
        g = hd // (N_HEADS // N_KV_HEADS)
        qp = q_ref[:, (hd // 2) * LANES:(hd // 2 + 1) * LANES]
        st_ref[hd % ATTN_SLOTS] = lax.dot_general(kp_ref[2 * g + hd % 2], qp, (((1,), (1,)), ((), ())),
                                                  preferred_element_type=F32)

    heads = []
    for hd in range(ATTN_SLOTS - 1):
        scores_t(hd)
    for hd in range(N_HEADS):
        g = hd // (N_HEADS // N_KV_HEADS)
        if hd + ATTN_SLOTS - 1 < N_HEADS:
            scores_t(hd + ATTN_SLOTS - 1)
        slot = hd % ATTN_SLOTS
        m = jnp.max(st_ref[slot], axis=0, keepdims=True)
        p = jnp.exp2(st_ref[slot] - m).astype(BF16)
        r = jnp.dot(va_ref[g], p, preferred_element_type=F32)
        num = r[g * HEAD_DIM:(g + 1) * HEAD_DIM]
        den = r[(1 - g) * HEAD_DIM:(1 - g) * HEAD_DIM + 1]
        heads.append(num / den)
    y = jnp.concatenate(heads, axis=0).T
    ms = jnp.mean(y * y, axis=-1, keepdims=True)
    o_ref[...] = (y * lax.rsqrt(ms + EPS) * nw_ref[...]).astype(BF16)


def _attention(q, kp, va, nw, batch, seq):
    tokens = q.shape[0]
    nq = seq // TQ_ATTN
    return pl.pallas_call(
        _attn_kernel,
        grid=(batch, nq),
        in_specs=[pl.BlockSpec((TQ_ATTN, D_ATTN), lambda b, j: (b * nq + j, 0)),
                  pl.BlockSpec((2 * N_KV_HEADS, seq, D_KV), lambda b, j: (0, b, 0)),
                  pl.BlockSpec((None, N_KV_HEADS, D_KV, seq), lambda b, j: (b, 0, 0, 0)),
                  pl.BlockSpec((1, D_ATTN), lambda b, j: (0, 0))],
        out_specs=pl.BlockSpec((TQ_ATTN, D_ATTN), lambda b, j: (b * nq + j, 0)),
        out_shape=jax.ShapeDtypeStruct((tokens, D_ATTN), BF16),
        scratch_shapes=[pltpu.VMEM((ATTN_SLOTS, seq, TQ_ATTN), F32)],
        compiler_params=_params(2),
        name="attn",
    )(q, kp, va, nw)


HG_LEVELS = (32, 16, 8)
N_EXP_GROUPS = 2 + 2 * (len(HG_LEVELS) + 1)


def _hgrn_constants():
    n = HG_CHUNK
    mats = []
    idx = np.arange(n)
    cum = (idx[None, :] <= idx[:, None]).astype(np.float32)
    last = (idx[None, :] > idx[:, None]).astype(np.float32)
    mats += [cum, last]
    for half in HG_LEVELS:
        mq = np.zeros((n, n), np.float32)
        mk = np.zeros((n, n), np.float32)
        for p in range(n):
            if (p // half) % 2 == 1:
                p0 = (p // half) * half
                mq[p, p0 + 1:p + 1] = 1.0
            else:
                p0 = (p // half) * half + half
                mk[p, p + 1:p0 + 1] = 1.0
        mats += [mq, mk]
    mq = np.zeros((n, n), np.float32)
    mk = np.zeros((n, n), np.float32)
    for p in range(n):
        pr = (p // HG_FINE) * HG_FINE + HG_REF
        if p > pr:
            mq[p, pr + 1:p + 1] = 1.0
            mk[p, pr + 1:p + 1] = -1.0
        elif p < pr:
            mq[p, p + 1:pr + 1] = -1.0
            mk[p, p + 1:pr + 1] = 1.0
    mats += [mq, mk]
    fwd = np.concatenate(mats, axis=0)
    bwd = np.concatenate([m[::-1, ::-1] for m in mats], axis=0)
    m3 = np.stack([np.tile(fwd, (1, 3)), np.tile(bwd, (1, 3))])

    lev = np.full((n, n), len(HG_LEVELS) + 1, np.int32)
    for t in range(n):
        for s in range(t + 1):
            if t // HG_FINE == s // HG_FINE:
                lev[t, s] = len(HG_LEVELS)
            else:
                for li, half in enumerate(HG_LEVELS):
                    if t // (2 * half) == s // (2 * half) and t // half != s // half:
                        lev[t, s] = li
                        break
    lev2 = np.stack([np.tile(lev, (1, HGRN_HEADS)), np.tile(lev[::-1, ::-1], (1, HGRN_HEADS))])
    head = np.arange(D_HGRN) // HEAD_DIM
    headmask = (head[:, None] == head[None, :]).astype(np.float32)
    return m3, lev2, headmask


def _hgrn_kernel(hq_ref, zf_ref, zb_ref, hi_ref, gate_ref, logit_ref, gnw_ref, m3_ref, lev_ref,
                 hm_ref, hmb_ref, seg_ref, o_ref, ofw_ref, obw_ref, st_ref, *, layer):
    seq = hq_ref.shape[0]
    n_chunks = seq // HG_CHUNK
    n = HG_CHUNK

    lg = [logit_ref[i] for i in range(DEPTH)]
    mx = functools.reduce(jnp.maximum, lg)
    ex = [jnp.exp(v - mx) for v in lg]
    den = functools.reduce(lambda a, b: a + b, ex)
    sm = [e / den for e in ex]
    acc = sm[0]
    for i in range(1, layer + 1):
        acc = acc + sm[i]
    lb = acc - sm[0]

    def chunk(d, z_ref, lb_row, r0):
        hm_bf = hmb_ref[...]
        z = z_ref[pl.ds(r0, n), :]
        q = hq_ref[pl.ds(r0, n), :]
        v = hi_ref[pl.ds(r0, n), :]
        f = lb_row + (1.0 - lb_row) * jax.nn.sigmoid(z)
        g = jnp.log(jnp.maximum(f, F_MIN))
        kk = (1.0 - lb_row) * jax.nn.sigmoid(-z)
        g1 = g.astype(BF16)
        r1 = g - g1.astype(F32)
        g2 = r1.astype(BF16)
        g3 = (r1 - g2.astype(F32)).astype(BF16)
        gcat = jnp.concatenate([g1, g2, g3], axis=0)
        expo = jnp.dot(m3_ref[d], gcat, preferred_element_type=F32)
        ex_all = jnp.exp(expo)
        x_cum = ex_all[0:n]
        x_last = ex_all[n:2 * n]
        decay_tot = x_cum[n - 1:n] if d == 0 else x_cum[0:1]

        lev = lev_ref[d]
        a = jnp.zeros((n, D_HGRN), F32)
        for li in range(len(HG_LEVELS) + 1):
            base = (2 + 2 * li) * n
            qt = (q * ex_all[base:base + n]).astype(BF16)
            kt = (kk * ex_all[base + n:base + 2 * n]).astype(BF16)
            kbd = jnp.concatenate([kt] * HGRN_HEADS, axis=0) * hm_bf
            sc = lax.dot_general(qt, kbd, (((1,), (1,)), ((), ())), preferred_element_type=F32)
            a = jnp.where(lev == li, sc, a)
        vbd = jnp.concatenate([v.astype(BF16)] * HGRN_HEADS, axis=0) * hm_bf
        o = jnp.dot(a.astype(BF16), vbd, preferred_element_type=F32)

        st = st_ref[d]
        qb = (q * x_cum).astype(BF16)
        o = o + lax.dot_general(qb, st.astype(BF16), (((1,), (1,)), ((), ())), preferred_element_type=F32)

        kdec = (kk * x_last).astype(BF16)
        zpad = jnp.zeros((LANES - n, D_HGRN), F32)
        vt = jnp.concatenate([v, zpad], axis=0).T.astype(BF16)
        kpad = jnp.concatenate([kdec, jnp.zeros((LANES - n, D_HGRN), BF16)], axis=0)
        upd = jnp.dot(vt, kpad, preferred_element_type=F32)
        st_ref[d] = (st * decay_tot + upd) * hm_ref[...]
        return o

    st_ref[...] = jnp.zeros_like(st_ref)

    def scan_body(ci, carry):
        rf = pl.multiple_of(ci * n, n)
        rb = pl.multiple_of((n_chunks - 1 - ci) * n, n)
        ofw_ref[pl.ds(rf, n), :] = chunk(0, zf_ref, lb[0:1], rf)
        obw_ref[pl.ds(rb, n), :] = chunk(1, zb_ref, lb[1:2], rb)
        return carry

    lax.fori_loop(0, n_chunks, scan_body, 0, unroll=2)

    gnw = gnw_ref[...]

    def finish_body(i, carry):
        r0 = pl.multiple_of(i * HG_FINISH_ROWS, HG_FINISH_ROWS)
        o = ofw_ref[pl.ds(r0, HG_FINISH_ROWS), :] + obw_ref[pl.ds(r0, HG_FINISH_ROWS), :]
        on = o * lax.rsqrt(_segment_mean_sq(o, seg_ref, HEAD_DIM) + EPS) * gnw
        gate = gate_ref[pl.ds(r0, HG_FINISH_ROWS), :]
        o_ref[pl.ds(r0, HG_FINISH_ROWS), :] = (on * (gate * jax.nn.sigmoid(gate))).astype(BF16)
        return carry

    lax.fori_loop(0, seq // HG_FINISH_ROWS, finish_body, 0)


def _hgrn(hg, logits, gnw, consts, seg, layer, batch, seq):
    tokens = hg.shape[0]
    m3, lev, hm, hmb = consts
    col = lambda j: pl.BlockSpec((seq, D_HGRN), lambda b: (b, j))
    return pl.pallas_call(
        functools.partial(_hgrn_kernel, layer=layer),
        grid=(batch,),
        in_specs=[col(0), col(1), col(2), col(3), col(4),
                  _full(logits.shape), _full((1, D_HGRN)), _full(m3.shape), _full(lev.shape),
                  _full(hm.shape), _full(hmb.shape), _full(seg.shape)],
        out_specs=pl.BlockSpec((seq, D_HGRN), lambda b: (b, 0)),
        out_shape=jax.ShapeDtypeStruct((tokens, D_HGRN), BF16),
        scratch_shapes=[pltpu.VMEM((seq, D_HGRN), F32), pltpu.VMEM((seq, D_HGRN), F32),
                        pltpu.VMEM((2, D_HGRN, D_HGRN), F32)],
        compiler_params=_params(1),
        name="hgrn",
    )(hg, hg, hg, hg, hg, logits, gnw, m3, lev, hm, hmb, seg)


def _conv_kernel(a_ref, b_ref, dww_ref, dwb_ref, lnw_ref, lnb_ref, pww_ref, pwb_ref, nw_ref,
                 o_ref, pad_ref, dw_ref):
    seq = a_ref.shape[0]
    halo = jnp.zeros((CONV_HALO, D_CONV), F32)
    pad_ref[0:CONV_HALO, :] = halo
    pad_ref[CONV_HALO + seq:CONV_HALO + seq + CONV_HALO, :] = halo
    pad_ref[CONV_HALO:CONV_HALO + seq, :] = a_ref[...] * jax.nn.sigmoid(b_ref[...])

    win_rows = CONV_ROWS + 2 * CONV_HALO

    def tap_body(i, carry):
        r0 = pl.multiple_of(i * CONV_ROWS, CONV_ROWS)
        win = pad_ref[pl.ds(r0, win_rows), :]
        acc = jnp.zeros((CONV_ROWS, D_CONV), F32)
        for r in range(SUBLANES):
            shifted = win if r == 0 else pltpu.roll(win, win_rows - r, 0)
            for a8 in range(0, 2 * CONV_HALO, SUBLANES):
                tap = a8 + r - (CONV_HALO - CONV_PAD)
                if 0 <= tap < CONV_WIDTH:
                    acc = acc + shifted[a8:a8 + CONV_ROWS] * dww_ref[tap:tap + 1, :]
        dw_ref[pl.ds(r0, CONV_ROWS), :] = acc + dwb_ref[...]
        return carry

    lax.fori_loop(0, seq // CONV_ROWS, tap_body, 0, unroll=2)

    def post_body(i, carry):
        r0 = pl.multiple_of(i * CONV_POST_ROWS, CONV_POST_ROWS)
        u = dw_ref[pl.ds(r0, CONV_POST_ROWS), :]
        mu = jnp.mean(u, axis=-1, keepdims=True)
        uc = u - mu
        var = jnp.mean(uc * uc, axis=-1, keepdims=True)
        y = uc * lax.rsqrt(var + LN_EPS) * lnw_ref[...] + lnb_ref[...]
        y = y * jax.nn.sigmoid(y)
        y = jnp.dot(y.astype(BF16), pww_ref[...], preferred_element_type=F32) + pwb_ref[...]
        ms = jnp.mean(y * y, axis=-1, keepdims=True)
        o_ref[pl.ds(r0, CONV_POST_ROWS), :] = (y * lax.rsqrt(ms + EPS) * nw_ref[...]).astype(BF16)
        return carry

    lax.fori_loop(0, seq // CONV_POST_ROWS, post_body, 0, unroll=2)


def _conv(cv, dww, dwb, lnw, lnb, pww, pwb, nw, batch, seq):
    tokens = cv.shape[0]
    col = lambda j: pl.BlockSpec((seq, D_CONV), lambda b: (b, j))
    vec = _full((1, D_CONV))
    return pl.pallas_call(
        _conv_kernel,
        grid=(batch,),
        in_specs=[col(0), col(1), _full(dww.shape), vec, vec, vec, _full((D_CONV, D_CONV)), vec, vec],
        out_specs=pl.BlockSpec((seq, D_CONV), lambda b: (b, 0)),
        out_shape=jax.ShapeDtypeStruct((tokens, D_CONV), BF16),
        scratch_shapes=[pltpu.VMEM((seq + 2 * CONV_HALO, D_CONV), F32), pltpu.VMEM((seq, D_CONV), F32)],
        compiler_params=_params(1),
        name="conv",
    )(cv, cv, dww, dwb, lnw, lnb, pww, pwb, nw)


def _ffn_kernel(x_ref, ya_ref, yh_ref, yc_ref, wo_ref, nw_ref, wg_ref, wu_ref, wd_ref,
                o_ref, act_ref):
    x1 = x_ref[...]
    x1 = x1 + jnp.dot(ya_ref[...], wo_ref[0:D_ATTN, :], preferred_element_type=F32)
    x1 = x1 + jnp.dot(yh_ref[...], wo_ref[D_ATTN:D_ATTN + D_HGRN, :], preferred_element_type=F32)
    x1 = x1 + jnp.dot(yc_ref[...], wo_ref[D_ATTN + D_HGRN:D_MODEL, :], preferred_element_type=F32)
    ms = jnp.mean(x1 * x1, axis=-1, keepdims=True)
    h = (x1 * lax.rsqrt(ms + EPS) * nw_ref[...]).astype(BF16)
    for c in range(0, D_FF, FF_CHUNK):
        g = jnp.dot(h, wg_ref[:, c:c + FF_CHUNK], preferred_element_type=F32)
        u = jnp.dot(h, wu_ref[:, c:c + FF_CHUNK], preferred_element_type=F32)
        act_ref[:, c:c + FF_CHUNK] = (g * jax.nn.sigmoid(g) * u).astype(BF16)
    o_ref[...] = x1 + jnp.dot(act_ref[...], wd_ref[...], preferred_element_type=F32)


def _ffn(x2, ya, yh, yc, wo, nw, wg, wu, wd):
    tokens = x2.shape[0]
    row = lambda w: pl.BlockSpec((TM_FFN, w), lambda i: (i, 0))
    resident = lambda shape: pl.BlockSpec(shape, lambda i: (0, 0), pipeline_mode=pl.Buffered(1))
    return pl.pallas_call(
        _ffn_kernel,
        grid=(tokens // TM_FFN,),
        in_specs=[row(D_MODEL), row(D_ATTN), row(D_HGRN), row(D_CONV),
                  resident((D_MODEL, D_MODEL)), _full((1, D_MODEL)),
                  resident((D_MODEL, D_FF)), resident((D_MODEL, D_FF)), resident((D_FF, D_MODEL))],
        out_specs=row(D_MODEL),
        out_shape=jax.ShapeDtypeStruct((tokens, D_MODEL), F32),
        scratch_shapes=[pltpu.VMEM((TM_FFN, D_FF), BF16)],
        compiler_params=_params(1),
        name="ffn",
    )(x2, ya, yh, yc, wo, nw, wg, wu, wd)


def _rope_tables(seq):
    rows = seq // GRID_W
    row_id = jnp.repeat(jnp.arange(rows, dtype=F32), GRID_W)
    col_id = jnp.tile(jnp.arange(GRID_W, dtype=F32), rows)
    half = HEAD_DIM // 2
    inv_freq = ROPE_THETA ** (-jnp.arange(0, half, 2, dtype=F32) / half)
    ang_r = row_id[:, None] * inv_freq[None, :]
    ang_c = col_id[:, None] * inv_freq[None, :]
    ang = jnp.concatenate([ang_r, ang_r, ang_c, ang_c], axis=-1)
    cos, sin = jnp.cos(ang), jnp.sin(ang)
    lane = np.arange(HEAD_DIM)
    sign = jnp.asarray(np.where((lane % (HEAD_DIM // 2)) < HEAD_DIM // 4, -1.0, 1.0), F32)
    sin_signed = sin * sign[None, :]
    scale = HEAD_DIM ** -0.5 * np.log2(np.e)
    return (jnp.tile(cos, (1, N_HEADS)) * scale, jnp.tile(sin_signed, (1, N_HEADS)) * scale,
            jnp.tile(cos, (1, N_KV_HEADS)), jnp.tile(sin_signed, (1, N_KV_HEADS)))


def _segment_matrix(width, seg_len):
    seg = np.arange(width) // seg_len
    return jnp.asarray((seg[:, None] == seg[None, :]).astype(np.float32), BF16)


def kernel(x, mix_norm_w, w_in, q_norm_w, k_norm_w, hgrn_lb_logits, hgrn_gnorm_w, conv_dw_w, conv_dw_b,
           conv_ln_w, conv_ln_b, conv_pw_w, conv_pw_b, attn_out_norm_w, conv_out_norm_w, w_out,
           ffn_norm_w, w_gate, w_up, w_down):
    batch, seq, d_model = x.shape
    assert d_model == D_MODEL and seq % TM_PROJ == 0 and seq % TQ_ATTN == 0 and seq % HG_CHUNK == 0
    assert (batch * seq) % TM_FFN == 0 and w_in.shape == (DEPTH, D_MODEL, D_IN_PROJ)
    assert w_gate.shape == (DEPTH, D_MODEL, D_FF)

    tabs = _rope_tables(seq)
    segq = _segment_matrix(D_ATTN, HEAD_DIM)
    segk = _segment_matrix(D_KV, HEAD_DIM)
    segh = _segment_matrix(D_HGRN, HEAD_DIM)
    m3, lev, hm = _hgrn_constants()
    hconsts = (jnp.asarray(m3, BF16), jnp.asarray(lev), jnp.asarray(hm), jnp.asarray(hm, BF16))
    logits = hgrn_lb_logits.astype(F32)

    x2 = x.reshape(batch * seq, D_MODEL)
    for l in range(DEPTH):
        q, kp, va, hg, cv = _inproj(
            x2, mix_norm_w[l][None, :], w_in[l].astype(BF16),
            jnp.tile(q_norm_w[l], N_HEADS)[None, :], jnp.tile(k_norm_w[l], N_KV_HEADS)[None, :],
            tabs, segq, segk, batch, seq)
        ya = _attention(q, kp, va, attn_out_norm_w[l][None, :], batch, seq)
        yh = _hgrn(hg, logits, jnp.tile(hgrn_gnorm_w[l], HGRN_HEADS)[None, :], hconsts, segh, l, batch, seq)
        yc = _conv(cv, conv_dw_w[l], conv_dw_b[l][None, :], conv_ln_w[l][None, :], conv_ln_b[l][None, :],
                   conv_pw_w[l].astype(BF16), conv_pw_b[l][None, :], conv_out_norm_w[l][None, :], batch, seq)
        x2 = _ffn(x2, ya, yh, yc, w_out[l].astype(BF16), ffn_norm_w[l][None, :],
                  w_gate[l].astype(BF16), w_up[l].astype(BF16), w_down[l].astype(BF16))
    return x2.reshape(batch, seq, D_MODEL)
```

```python
import functools

import numpy as np
import jax
import jax.numpy as jnp
from jax import lax
from jax.experimental import pallas as pl
from jax.experimental.pallas import tpu as pltpu

F32 = jnp.float32
BF16 = jnp.bfloat16

D_MODEL = 1024
DEPTH = 2
GRID_W = 64
D_ATTN = 512
D_HGRN = 256
D_CONV = 256
HEAD_DIM = 64
N_HEADS = D_ATTN // HEAD_DIM
N_KV_HEADS = 2
D_KV = N_KV_HEADS * HEAD_DIM
ROPE_THETA = 10000.0
HGRN_HEADS = D_HGRN // HEAD_DIM
F_MIN = 1e-6
LOG2_E = float(np.log2(np.e))
CONV_WIDTH = 31
CONV_PAD = (CONV_WIDTH - 1) // 2
D_FF = 2816
EPS = 1e-6
LN_EPS = 1e-5
D_IN_PROJ = D_ATTN + 2 * D_KV + 5 * D_HGRN + 2 * D_CONV

OFF_Q = 0
OFF_K = OFF_Q + D_ATTN
OFF_V = OFF_K + D_KV
OFF_H = OFF_V + D_KV
OFF_C = OFF_H + 5 * D_HGRN

LANES = 128
SUBLANES = 8
VMEM_LIMIT_BYTES = 56 * 1024 * 1024

TM_PROJ = 512
TQ_ATTN = 256
ATTN_SUBTILES = 2
ATTN_SLOTS = 4
TM_FFN = 512
FF_CHUNK = 256
HG_CHUNK = 64
HG_FINE = 8
HG_REF = 2
HG_FINISH_ROWS = 256
HG_UNROLL = 4
CONV_ROWS = 64
CONV_POST_ROWS = 256
CONV_HALO = 16

assert -np.log(F_MIN) * (HG_FINE - 1 - HG_REF) < 80.0 and -np.log(F_MIN) * HG_REF < 80.0


def _params(n_grid_axes):
    return pltpu.CompilerParams(
        dimension_semantics=("arbitrary",) * n_grid_axes,
        vmem_limit_bytes=VMEM_LIMIT_BYTES)


def _full(shape):
    nd = len(shape)
    return pl.BlockSpec(shape, lambda *_: (0,) * nd)


def _segment_mean_sq(y, seg_ref, seg_len):
    return jnp.dot((y * y).astype(BF16), seg_ref[...], preferred_element_type=F32) * (1.0 / seg_len)


def _rope(xn, cos, sin_signed, first_mask):
    width = xn.shape[1]
    left = pltpu.roll(xn, width - HEAD_DIM // 4, 1)
    right = pltpu.roll(xn, HEAD_DIM // 4, 1)
    rot = jnp.where(first_mask, left, right)
    return xn * cos + rot * sin_signed


def _inproj_kernel(x_ref, nw_ref, w_ref, qnw_ref, knw_ref, cq_ref, sq_ref, ck_ref, sk_ref,
                   segq_ref, segk_ref,
                   q_out, kp_out, va_out, hg_out, cv_out):
    x = x_ref[...]
    ms = jnp.mean(x * x, axis=-1, keepdims=True)
    h = (x * lax.rsqrt(ms + EPS) * nw_ref[...]).astype(BF16)

    q = jnp.dot(h, w_ref[:, OFF_Q:OFF_K], preferred_element_type=F32)
    qn = q * lax.rsqrt(_segment_mean_sq(q, segq_ref, HEAD_DIM) + EPS) * qnw_ref[...]
    lane_q = lax.broadcasted_iota(jnp.int32, (1, D_ATTN), 1)
    first_q = (lane_q & (HEAD_DIM // 2 - 1)) < HEAD_DIM // 4
    q_out[...] = _rope(qn, cq_ref[...], sq_ref[...], first_q).astype(BF16)

    k = jnp.dot(h, w_ref[:, OFF_K:OFF_V], preferred_element_type=F32)
    kn = k * lax.rsqrt(_segment_mean_sq(k, segk_ref, HEAD_DIM) + EPS) * knw_ref[...]
    lane_k = lax.broadcasted_iota(jnp.int32, (1, D_KV), 1)
    first_k = (lane_k & (HEAD_DIM // 2 - 1)) < HEAD_DIM // 4
    kr = _rope(kn, ck_ref[...], sk_ref[...], first_k)
    ks = pltpu.roll(kr, HEAD_DIM, 1)
    low = lane_k < HEAD_DIM
    zero = jnp.zeros_like(kr)
    kp_out[0] = jnp.where(low, kr, zero).astype(BF16)
    kp_out[1] = jnp.where(low, zero, ks).astype(BF16)
    kp_out[2] = jnp.where(low, ks, zero).astype(BF16)
    kp_out[3] = jnp.where(low, zero, kr).astype(BF16)

    v = jnp.dot(h, w_ref[:, OFF_V:OFF_H], preferred_element_type=F32)
    vt = v.T
    ones = jnp.ones((HEAD_DIM, vt.shape[1]), F32)
    va_out[0] = jnp.concatenate([vt[0:HEAD_DIM], ones], axis=0).astype(BF16)
    va_out[1] = jnp.concatenate([ones, vt[HEAD_DIM:D_KV]], axis=0).astype(BF16)

    hg_out[...] = jnp.dot(h, w_ref[:, OFF_H:OFF_C], preferred_element_type=F32)
    cv_out[...] = jnp.dot(h, w_ref[:, OFF_C:D_IN_PROJ], preferred_element_type=F32)


def _inproj(x2, nw, w_in, qnw, knw, tabs, segq, segk, batch, seq):
    tokens = x2.shape[0]
    n_seq_tiles = seq // TM_PROJ
    cq, sq, ck, sk = tabs
    tab_q = pl.BlockSpec((TM_PROJ, D_ATTN), lambda i: (i % n_seq_tiles, 0))
    tab_k = pl.BlockSpec((TM_PROJ, D_KV), lambda i: (i % n_seq_tiles, 0))
    row = lambda w: pl.BlockSpec((TM_PROJ, w), lambda i: (i, 0))
    return pl.pallas_call(
        _inproj_kernel,
        grid=(tokens // TM_PROJ,),
        in_specs=[row(D_MODEL), _full((1, D_MODEL)), _full((D_MODEL, D_IN_PROJ)),
                  _full((1, D_ATTN)), _full((1, D_KV)), tab_q, tab_q, tab_k, tab_k,
                  _full((D_ATTN, D_ATTN)), _full((D_KV, D_KV))],
        out_specs=[row(D_ATTN),
                   pl.BlockSpec((2 * N_KV_HEADS, TM_PROJ, D_KV), lambda i: (0, i, 0)),
                   pl.BlockSpec((None, N_KV_HEADS, D_KV, TM_PROJ),
                                lambda i: (i // n_seq_tiles, 0, 0, i % n_seq_tiles)),
                   row(5 * D_HGRN), row(2 * D_CONV)],
        out_shape=[jax.ShapeDtypeStruct((tokens, D_ATTN), BF16),
                   jax.ShapeDtypeStruct((2 * N_KV_HEADS, tokens, D_KV), BF16),
                   jax.ShapeDtypeStruct((batch, N_KV_HEADS, D_KV, seq), BF16),
                   jax.ShapeDtypeStruct((tokens, 5 * D_HGRN), F32),
                   jax.ShapeDtypeStruct((tokens, 2 * D_CONV), F32)],
        compiler_params=_params(1),
        name="inproj",
    )(x2, nw, w_in, qnw, knw, cq, sq, ck, sk, segq, segk)


def _attn_kernel(q_ref, kp_ref, va_ref, nw_ref, o_ref, st_ref):
    n_items = ATTN_SUBTILES * N_HEADS

    def scores_t(it):
        sub, hd = divmod(it, N_HEADS)
        g = hd // (N_HEADS // N_KV_HEADS)
        qp = q_ref[sub * TQ_ATTN:(sub + 1) * TQ_ATTN, (hd // 2) * LANES:(hd // 2 + 1) * LANES]
        st_ref[it % ATTN_SLOTS] = lax.dot_general(kp_ref[2 * g + hd % 2], qp, (((1,), (1,)), ((), ())),
                                                  preferred_element_type=F32)

    heads = []
    for it in range(ATTN_SLOTS - 1):
        scores_t(it)
    for it in range(n_items):
        sub, hd = divmod(it, N_HEADS)
        g = hd // (N_HEADS // N_KV_HEADS)
        if it + ATTN_SLOTS - 1 < n_items:
            scores_t(it + ATTN_SLOTS - 1)
        slot = it % ATTN_SLOTS
        m = jnp.max(st_ref[slot], axis=0, keepdims=True)
        p = jnp.exp2(st_ref[slot] - m).astype(BF16)
        r = jnp.dot(va_ref[g], p, preferred_element_type=F32)
        num = r[g * HEAD_DIM:(g + 1) * HEAD_DIM]
        den = r[(1 - g) * HEAD_DIM:(1 - g) * HEAD_DIM + 1]
        heads.append(num / den)
        if hd == N_HEADS - 1:
            y = jnp.concatenate(heads, axis=0).T
            heads = []
            ms = jnp.mean(y * y, axis=-1, keepdims=True)
            o_ref[sub * TQ_ATTN:(sub + 1) * TQ_ATTN, :] = (y * lax.rsqrt(ms + EPS) * nw_ref[...]).astype(BF16)


def _attention(q, kp, va, nw, batch, seq):
    tokens = q.shape[0]
    rows = ATTN_SUBTILES * TQ_ATTN
    nq = seq // rows
    return pl.pallas_call(
        _attn_kernel,
        grid=(batch, nq),
        in_specs=[pl.BlockSpec((rows, D_ATTN), lambda b, j: (b * nq + j, 0)),
                  pl.BlockSpec((2 * N_KV_HEADS, seq, D_KV), lambda b, j: (0, b, 0)),
                  pl.BlockSpec((None, N_KV_HEADS, D_KV, seq), lambda b, j: (b, 0, 0, 0)),
                  pl.BlockSpec((1, D_ATTN), lambda b, j: (0, 0))],
        out_specs=pl.BlockSpec((rows, D_ATTN), lambda b, j: (b * nq + j, 0)),
        out_shape=jax.ShapeDtypeStruct((tokens, D_ATTN), BF16),
        scratch_shapes=[pltpu.VMEM((ATTN_SLOTS, seq, TQ_ATTN), F32)],
        compiler_params=_params(2),
        name="attn",
    )(q, kp, va, nw)


HG_LEVELS = (32, 16, 8)
N_EXP_GROUPS = 2 + len(HG_LEVELS)


def _hgrn_constants():
    n = HG_CHUNK
    mats = []
    idx = np.arange(n)
    cum = (idx[None, :] <= idx[:, None]).astype(np.float32)
    mats.append(cum)
    for half in HG_LEVELS:
        m = np.zeros((n, n), np.float32)
        for p in range(n):
            if (p // half) % 2 == 1:
                p0 = (p // half) * half
                m[p, p0 + 1:p + 1] = 1.0
            else:
                p0 = (p // half) * half + half
                m[p, p + 1:p0 + 1] = 1.0
        mats.append(m)
    m = np.zeros((n, n), np.float32)
    for p in range(n):
        pr = (p // HG_FINE) * HG_FINE + HG_REF
        if p > pr:
            m[p, pr + 1:p + 1] = 1.0
        elif p < pr:
            m[p, p + 1:pr + 1] = -1.0
    mats.append(m)
    fwd = np.concatenate(mats, axis=0)
    bwd = np.concatenate([m[::-1, ::-1] for m in mats], axis=0)
    m2 = np.stack([np.tile(fwd, (1, 2)), np.tile(bwd, (1, 2))])

    lev = np.full((n, n), len(HG_LEVELS) + 1, np.int32)
    for t in range(n):
        for s in range(t + 1):
            if t // HG_FINE == s // HG_FINE:
                lev[t, s] = len(HG_LEVELS)
            else:
                for li, half in enumerate(HG_LEVELS):
                    if t // (2 * half) == s // (2 * half) and t // half != s // half:
                        lev[t, s] = li
                        break
    lev2 = np.stack([np.tile(lev, (1, HGRN_HEADS)), np.tile(lev[::-1, ::-1], (1, HGRN_HEADS))])
    head = np.arange(D_HGRN) // HEAD_DIM
    headmask = (head[:, None] == head[None, :]).astype(np.float32)
    return m2, lev2, headmask


def _hgrn_kernel(hq_ref, zf_ref, zb_ref, hi_ref, gate_ref, logit_ref, gnw_ref, m2_ref, lev_ref,
                 hmb_ref, seg_ref, o_ref, ofw_ref, obw_ref, st_ref, *, layer):
    seq = hq_ref.shape[0]
    n_chunks = seq // HG_CHUNK
    n = HG_CHUNK

    lg = [logit_ref[i] for i in range(DEPTH)]
    mx = functools.reduce(jnp.maximum, lg)
    ex = [jnp.exp(v - mx) for v in lg]
    den = functools.reduce(lambda a, b: a + b, ex)
    sm = [e / den for e in ex]
    acc = sm[0]
    for i in range(1, layer + 1):
        acc = acc + sm[i]
    lb = acc - sm[0]

    def stage_a(d, z_ref, lb_row, r0):
        z = z_ref[pl.ds(r0, n), :]
        f = lb_row + (1.0 - lb_row) * jax.nn.sigmoid(z)
        g = jnp.log(jnp.maximum(f, F_MIN)) * LOG2_E
        kk = (1.0 - lb_row) * jax.nn.sigmoid(-z)
        g1 = g.astype(BF16)
        g2 = (g - g1.astype(F32)).astype(BF16)
        gcat = jnp.concatenate([g1, g2], axis=0)
        expo = jnp.dot(m2_ref[d], gcat, preferred_element_type=F32)
        return kk, expo

    def stage_b(d, r0, kk, expo):
        hm_bf = hmb_ref[...]
        q = hq_ref[pl.ds(r0, n), :]
        v = hi_ref[pl.ds(r0, n), :]
        lev = lev_ref[d]
        a = jnp.zeros((n, D_HGRN), F32)
        for li in range(len(HG_LEVELS) + 1):
            e_lev = expo[(1 + li) * n:(2 + li) * n]
            x_q = jnp.exp2(e_lev)
            x_k = x_q if li < len(HG_LEVELS) else jnp.exp2(-e_lev)
            qt = (q * x_q).astype(BF16)
            kt = (kk * x_k).astype(BF16)
            kbd = jnp.concatenate([kt] * HGRN_HEADS, axis=0) * hm_bf
            sc = lax.dot_general(qt, kbd, (((1,), (1,)), ((), ())), preferred_element_type=F32)
            a = jnp.where(lev == li, sc, a)
        vbd = jnp.concatenate([v.astype(BF16)] * HGRN_HEADS, axis=0) * hm_bf
        o_intra = jnp.dot(a.astype(BF16), vbd, preferred_element_type=F32)

        e_cum = expo[0:n]
        e_tot = e_cum[n - 1:n] if d == 0 else e_cum[0:1]
        qb = (q * jnp.exp2(e_cum)).astype(BF16)
        kdec = (kk * jnp.exp2(e_tot - e_cum)).astype(BF16)
        zpad = jnp.zeros((LANES - n, D_HGRN), F32)
        vt = jnp.concatenate([v, zpad], axis=0).T.astype(BF16)
        kpad = jnp.concatenate([kdec, jnp.zeros((LANES - n, D_HGRN), BF16)], axis=0)
        upd = jnp.dot(vt, kpad, preferred_element_type=F32)
        return o_intra, qb, upd, jnp.exp2(e_tot)

    def stage_c(d, o_intra, qb, upd, decay_tot):
        st = st_ref[d]
        o = o_intra + lax.dot_general(qb, st.astype(BF16) * hmb_ref[...], (((1,), (1,)), ((), ())),
                                      preferred_element_type=F32)
        st_ref[d] = st * decay_tot + upd
        return o

    st_ref[...] = jnp.zeros_like(st_ref)

    def scan_body(ci, carry):
        items = []
        for u in range(HG_UNROLL):
            cf = ci * HG_UNROLL + u
            items.append((0, zf_ref, lb[0:1], pl.multiple_of(cf * n, n), ofw_ref))
            items.append((1, zb_ref, lb[1:2], pl.multiple_of((n_chunks - 1 - cf) * n, n), obw_ref))
        sa = [stage_a(d, z_ref, lb_row, r0) for d, z_ref, lb_row, r0, _ in items]
        sb = [stage_b(d, r0, *a) for (d, _, _, r0, _), a in zip(items, sa)]
        for (d, _, _, r0, out_ref), b in zip(items, sb):
            out_ref[pl.ds(r0, n), :] = stage_c(d, *b)
        return carry

    lax.fori_loop(0, n_chunks // HG_UNROLL, scan_body, 0)

    gnw = gnw_ref[...]

    def finish_body(i, carry):
        r0 = pl.multiple_of(i * HG_FINISH_ROWS, HG_FINISH_ROWS)
        o = ofw_ref[pl.ds(r0, HG_FINISH_ROWS), :] + obw_ref[pl.ds(r0, HG_FINISH_ROWS), :]
        on = o * lax.rsqrt(_segment_mean_sq(o, seg_ref, HEAD_DIM) + EPS) * gnw
        gate = gate_ref[pl.ds(r0, HG_FINISH_ROWS), :]
        o_ref[pl.ds(r0, HG_FINISH_ROWS), :] = (on * (gate * jax.nn.sigmoid(gate))).astype(BF16)
        return carry

    lax.fori_loop(0, seq // HG_FINISH_ROWS, finish_body, 0)


def _hgrn(hg, logits, gnw, consts, seg, layer, batch, seq):
    tokens = hg.shape[0]
    m2, lev, hmb = consts
    col = lambda j: pl.BlockSpec((seq, D_HGRN), lambda b: (b, j))
    return pl.pallas_call(
        functools.partial(_hgrn_kernel, layer=layer),
        grid=(batch,),
        in_specs=[col(0), col(1), col(2), col(3), col(4),
                  _full(logits.shape), _full((1, D_HGRN)), _full(m2.shape), _full(lev.shape),
                  _full(hmb.shape), _full(seg.shape)],
        out_specs=pl.BlockSpec((seq, D_HGRN), lambda b: (b, 0)),
        out_shape=jax.ShapeDtypeStruct((tokens, D_HGRN), BF16),
        scratch_shapes=[pltpu.VMEM((seq, D_HGRN), F32), pltpu.VMEM((seq, D_HGRN), F32),
                        pltpu.VMEM((2, D_HGRN, D_HGRN), F32)],
        compiler_params=_params(1),
        name="hgrn",
    )(hg, hg, hg, hg, hg, logits, gnw, m2, lev, hmb, seg)


def _conv_kernel(a_ref, b_ref, dww_ref, dwb_ref, lnw_ref, lnb_ref, pww_ref, pwb_ref, nw_ref,
                 o_ref, pad_ref, dw_ref):
    seq = a_ref.shape[0]
    halo = jnp.zeros((CONV_HALO, D_CONV), F32)
    pad_ref[0:CONV_HALO, :] = halo
    pad_ref[CONV_HALO + seq:CONV_HALO + seq + CONV_HALO, :] = halo
    pad_ref[CONV_HALO:CONV_HALO + seq, :] = a_ref[...] * jax.nn.sigmoid(b_ref[...])

    win_rows = CONV_ROWS + 2 * CONV_HALO

    def tap_body(i, carry):
        r0 = pl.multiple_of(i * CONV_ROWS, CONV_ROWS)
        win = pad_ref[pl.ds(r0, win_rows), :]
        acc = jnp.zeros((CONV_ROWS, D_CONV), F32)
        for r in range(SUBLANES):
            shifted = win if r == 0 else pltpu.roll(win, win_rows - r, 0)
            for a8 in range(0, 2 * CONV_HALO, SUBLANES):
                tap = a8 + r - (CONV_HALO - CONV_PAD)
                if 0 <= tap < CONV_WIDTH:
                    acc = acc + shifted[a8:a8 + CONV_ROWS] * dww_ref[tap:tap + 1, :]
        dw_ref[pl.ds(r0, CONV_ROWS), :] = acc + dwb_ref[...]
        return carry

    lax.fori_loop(0, seq // CONV_ROWS, tap_body, 0, unroll=2)

    def post_body(i, carry):
        r0 = pl.multiple_of(i * CONV_POST_ROWS, CONV_POST_ROWS)
        u = dw_ref[pl.ds(r0, CONV_POST_ROWS), :]
        mu = jnp.mean(u, axis=-1, keepdims=True)
        uc = u - mu
        var = jnp.mean(uc * uc, axis=-1, keepdims=True)
        y = uc * lax.rsqrt(var + LN_EPS) * lnw_ref[...] + lnb_ref[...]
        y = y * jax.nn.sigmoid(y)
        y = jnp.dot(y.astype(BF16), pww_ref[...], preferred_element_type=F32) + pwb_ref[...]
        ms = jnp.mean(y * y, axis=-1, keepdims=True)
        o_ref[pl.ds(r0, CONV_POST_ROWS), :] = (y * lax.rsqrt(ms + EPS) * nw_ref[...]).astype(BF16)
        return carry

    lax.fori_loop(0, seq // CONV_POST_ROWS, post_body, 0, unroll=2)


def _conv(cv, dww, dwb, lnw, lnb, pww, pwb, nw, batch, seq):
    tokens = cv.shape[0]
    col = lambda j: pl.BlockSpec((seq, D_CONV), lambda b: (b, j))
    vec = _full((1, D_CONV))
    return pl.pallas_call(
        _conv_kernel,
        grid=(batch,),
        in_specs=[col(0), col(1), _full(dww.shape), vec, vec, vec, _full((D_CONV, D_CONV)), vec, vec],
        out_specs=pl.BlockSpec((seq, D_CONV), lambda b: (b, 0)),
        out_shape=jax.ShapeDtypeStruct((tokens, D_CONV), BF16),
        scratch_shapes=[pltpu.VMEM((seq + 2 * CONV_HALO, D_CONV), F32), pltpu.VMEM((seq, D_CONV), F32)],
        compiler_params=_params(1),
        name="conv",
    )(cv, cv, dww, dwb, lnw, lnb, pww, pwb, nw)


def _ffn_kernel(x_ref, ya_ref, yh_ref, yc_ref, wo_ref, nw_ref, wg_ref, wu_ref, wd_ref,
                o_ref, act_ref):
    x1 = x_ref[...]
    x1 = x1 + jnp.dot(ya_ref[...], wo_ref[0:D_ATTN, :], preferred_element_type=F32)
    x1 = x1 + jnp.dot(yh_ref[...], wo_ref[D_ATTN:D_ATTN + D_HGRN, :], preferred_element_type=F32)
    x1 = x1 + jnp.dot(yc_ref[...], wo_ref[D_ATTN + D_HGRN:D_MODEL, :], preferred_element_type=F32)
    ms = jnp.mean(x1 * x1, axis=-1, keepdims=True)
    h = (x1 * lax.rsqrt(ms + EPS) * nw_ref[...]).astype(BF16)
    for c in range(0, D_FF, FF_CHUNK):
        g = jnp.dot(h, wg_ref[:, c:c + FF_CHUNK], preferred_element_type=F32)
        u = jnp.dot(h, wu_ref[:, c:c + FF_CHUNK], preferred_element_type=F32)
        act_ref[:, c:c + FF_CHUNK] = (g * jax.nn.sigmoid(g) * u).astype(BF16)
    o_ref[...] = x1 + jnp.dot(act_ref[...], wd_ref[...], preferred_element_type=F32)


def _ffn(x2, ya, yh, yc, wo, nw, wg, wu, wd):
    tokens = x2.shape[0]
    row = lambda w: pl.BlockSpec((TM_FFN, w), lambda i: (i, 0))
    resident = lambda shape: pl.BlockSpec(shape, lambda i: (0, 0), pipeline_mode=pl.Buffered(1))
    return pl.pallas_call(
        _ffn_kernel,
        grid=(tokens // TM_FFN,),
        in_specs=[row(D_MODEL), row(D_ATTN), row(D_HGRN), row(D_CONV),
                  resident((D_MODEL, D_MODEL)), _full((1, D_MODEL)),
                  resident((D_MODEL, D_FF)), resident((D_MODEL, D_FF)), resident((D_FF, D_MODEL))],
        out_specs=row(D_MODEL),
        out_shape=jax.ShapeDtypeStruct((tokens, D_MODEL), F32),
        scratch_shapes=[pltpu.VMEM((TM_FFN, D_FF), BF16)],
        compiler_params=_params(1),
        name="ffn",
    )(x2, ya, yh, yc, wo, nw, wg, wu, wd)


def _rope_tables(seq):
    rows = seq // GRID_W
    row_id = jnp.repeat(jnp.arange(rows, dtype=F32), GRID_W)
    col_id = jnp.tile(jnp.arange(GRID_W, dtype=F32), rows)
    half = HEAD_DIM // 2
    inv_freq = ROPE_THETA ** (-jnp.arange(0, half, 2, dtype=F32) / half)
    ang_r = row_id[:, None] * inv_freq[None, :]
    ang_c = col_id[:, None] * inv_freq[None, :]
    ang = jnp.concatenate([ang_r, ang_r, ang_c, ang_c], axis=-1)
    cos, sin = jnp.cos(ang), jnp.sin(ang)
    lane = np.arange(HEAD_DIM)
    sign = jnp.asarray(np.where((lane % (HEAD_DIM // 2)) < HEAD_DIM // 4, -1.0, 1.0), F32)
    sin_signed = sin * sign[None, :]
    scale = HEAD_DIM ** -0.5 * np.log2(np.e)
    return (jnp.tile(cos, (1, N_HEADS)) * scale, jnp.tile(sin_signed, (1, N_HEADS)) * scale,
            jnp.tile(cos, (1, N_KV_HEADS)), jnp.tile(sin_signed, (1, N_KV_HEADS)))


def _segment_matrix(width, seg_len):
    seg = np.arange(width) // seg_len
    return jnp.asarray((seg[:, None] == seg[None, :]).astype(np.float32), BF16)


def kernel(x, mix_norm_w, w_in, q_norm_w, k_norm_w, hgrn_lb_logits, hgrn_gnorm_w, conv_dw_w, conv_dw_b,
           conv_ln_w, conv_ln_b, conv_pw_w, conv_pw_b, attn_out_norm_w, conv_out_norm_w, w_out,
           ffn_norm_w, w_gate, w_up, w_down):
    batch, seq, d_model = x.shape
    assert d_model == D_MODEL and seq % TM_PROJ == 0 and seq % HG_CHUNK == 0
    assert seq % (ATTN_SUBTILES * TQ_ATTN) == 0 and seq % CONV_POST_ROWS == 0 and seq % HG_FINISH_ROWS == 0
    assert (batch * seq) % TM_FFN == 0 and w_in.shape == (DEPTH, D_MODEL, D_IN_PROJ)
    assert w_gate.shape == (DEPTH, D_MODEL, D_FF)

    tabs = _rope_tables(seq)
    segq = _segment_matrix(D_ATTN, HEAD_DIM)
    segk = _segment_matrix(D_KV, HEAD_DIM)
    segh = _segment_matrix(D_HGRN, HEAD_DIM)
    m2, lev, hm = _hgrn_constants()
    hconsts = (jnp.asarray(m2, BF16), jnp.asarray(lev), jnp.asarray(hm, BF16))
    logits = hgrn_lb_logits.astype(F32)

    x2 = x.reshape(batch * seq, D_MODEL)
    for l in range(DEPTH):
        q, kp, va, hg, cv = _inproj(
            x2, mix_norm_w[l][None, :], w_in[l].astype(BF16),
            jnp.tile(q_norm_w[l], N_HEADS)[None, :], jnp.tile(k_norm_w[l], N_KV_HEADS)[None, :],
            tabs, segq, segk, batch, seq)
        ya = _attention(q, kp, va, attn_out_norm_w[l][None, :], batch, seq)
        yh = _hgrn(hg, logits, jnp.tile(hgrn_gnorm_w[l], HGRN_HEADS)[None, :], hconsts, segh, l, batch, seq)
        yc = _conv(cv, conv_dw_w[l], conv_dw_b[l][None, :], conv_ln_w[l][None, :], conv_ln_b[l][None, :],
                   conv_pw_w[l].astype(BF16), conv_pw_b[l][None, :], conv_out_norm_w[l][None, :], batch, seq)
        x2 = _ffn(x2, ya, yh, yc, w_out[l].astype(BF16), ffn_norm_w[l][None, :],
                  w_gate[l].astype(BF16), w_up[l].astype(BF16), w_down[l].astype(BF16))
    return x2.reshape(batch, seq, D_MODEL)
```

```python
import functools

import numpy as np
import jax
import jax.numpy as jnp
from jax import lax
from jax.experimental import pallas as pl
from jax.experimental.pallas import tpu as pltpu

F32 = jnp.float32
BF16 = jnp.bfloat16

D_MODEL = 1024
DEPTH = 2
GRID_W = 64
D_ATTN = 512
D_HGRN = 256
D_CONV = 256
HEAD_DIM = 64
N_HEADS = D_ATTN // HEAD_DIM
N_KV_HEADS = 2
D_KV = N_KV_HEADS * HEAD_DIM
ROPE_THETA = 10000.0
HGRN_HEADS = D_HGRN // HEAD_DIM
F_MIN = 1e-6
LOG2_E = float(np.log2(np.e))
CONV_WIDTH = 31
CONV_PAD = (CONV_WIDTH - 1) // 2
D_FF = 2816
EPS = 1e-6
LN_EPS = 1e-5
D_IN_PROJ = D_ATTN + 2 * D_KV + 5 * D_HGRN + 2 * D_CONV

OFF_Q = 0
OFF_K = OFF_Q + D_ATTN
OFF_V = OFF_K + D_KV
OFF_H = OFF_V + D_KV
OFF_C = OFF_H + 5 * D_HGRN

LANES = 128
SUBLANES = 8
VMEM_LIMIT_BYTES = 56 * 1024 * 1024

TM_PROJ = 512
TQ_ATTN = 256
ATTN_SUBTILES = 4
ATTN_AHEAD = 3
ATTN_SLOTS = ATTN_AHEAD + 1
TM_FFN = 512
FF_CHUNK = 256
HG_CHUNK = 64
HG_FINE = 8
HG_REF = 2
HG_FINISH_ROWS = 256
HG_UNROLL = 4
CONV_ROWS = 64
CONV_POST_ROWS = 256
CONV_HALO = 16

assert -np.log(F_MIN) * (HG_FINE - 1 - HG_REF) < 80.0 and -np.log(F_MIN) * HG_REF < 80.0


def _params(n_grid_axes):
    return pltpu.CompilerParams(
        dimension_semantics=("arbitrary",) * n_grid_axes,
        vmem_limit_bytes=VMEM_LIMIT_BYTES)


def _full(shape):
    nd = len(shape)
    return pl.BlockSpec(shape, lambda *_: (0,) * nd)


def _segment_mean_sq(y, seg_ref, seg_len):
    return jnp.dot((y * y).astype(BF16), seg_ref[...], preferred_element_type=F32) * (1.0 / seg_len)


def _rope(xn, cos, sin_signed, first_mask):
    width = xn.shape[1]
    left = pltpu.roll(xn, width - HEAD_DIM // 4, 1)
    right = pltpu.roll(xn, HEAD_DIM // 4, 1)
    rot = jnp.where(first_mask, left, right)
    return xn * cos + rot * sin_signed


def _inproj_kernel(x_ref, nw_ref, w_ref, qnw_ref, knw_ref, cq_ref, sq_ref, ck_ref, sk_ref,
                   segq_ref, segk_ref,
                   q_out, kp_out, va_out, hg_out, cv_out):
    x = x_ref[...]
    ms = jnp.mean(x * x, axis=-1, keepdims=True)
    h = (x * lax.rsqrt(ms + EPS) * nw_ref[...]).astype(BF16)

    q = jnp.dot(h, w_ref[:, OFF_Q:OFF_K], preferred_element_type=F32)
    qn = q * lax.rsqrt(_segment_mean_sq(q, segq_ref, HEAD_DIM) + EPS) * qnw_ref[...]
    lane_q = lax.broadcasted_iota(jnp.int32, (1, D_ATTN), 1)
    first_q = (lane_q & (HEAD_DIM // 2 - 1)) < HEAD_DIM // 4
    q_out[...] = _rope(qn, cq_ref[...], sq_ref[...], first_q).astype(BF16)

    k = jnp.dot(h, w_ref[:, OFF_K:OFF_V], preferred_element_type=F32)
    kn = k * lax.rsqrt(_segment_mean_sq(k, segk_ref, HEAD_DIM) + EPS) * knw_ref[...]
    lane_k = lax.broadcasted_iota(jnp.int32, (1, D_KV), 1)
    first_k = (lane_k & (HEAD_DIM // 2 - 1)) < HEAD_DIM // 4
    kr = _rope(kn, ck_ref[...], sk_ref[...], first_k)
    ks = pltpu.roll(kr, HEAD_DIM, 1)
    low = lane_k < HEAD_DIM
    zero = jnp.zeros_like(kr)
    kp_out[0] = jnp.where(low, kr, zero).astype(BF16)
    kp_out[1] = jnp.where(low, zero, ks).astype(BF16)
    kp_out[2] = jnp.where(low, ks, zero).astype(BF16)
    kp_out[3] = jnp.where(low, zero, kr).astype(BF16)

    v = jnp.dot(h, w_ref[:, OFF_V:OFF_H], preferred_element_type=F32)
    vt = v.T
    ones = jnp.ones((HEAD_DIM, vt.shape[1]), F32)
    va_out[0] = jnp.concatenate([vt[0:HEAD_DIM], ones], axis=0).astype(BF16)
    va_out[1] = jnp.concatenate([ones, vt[HEAD_DIM:D_KV]], axis=0).astype(BF16)

    hg_out[...] = jnp.dot(h, w_ref[:, OFF_H:OFF_C], preferred_element_type=F32)
    cv_out[...] = jnp.dot(h, w_ref[:, OFF_C:D_IN_PROJ], preferred_element_type=F32)


def _inproj(x2, nw, w_in, qnw, knw, tabs, segq, segk, batch, seq):
    tokens = x2.shape[0]
    n_seq_tiles = seq // TM_PROJ
    cq, sq, ck, sk = tabs
    tab_q = pl.BlockSpec((TM_PROJ, D_ATTN), lambda i: (i % n_seq_tiles, 0))
    tab_k = pl.BlockSpec((TM_PROJ, D_KV), lambda i: (i % n_seq_tiles, 0))
    row = lambda w: pl.BlockSpec((TM_PROJ, w), lambda i: (i, 0))
    return pl.pallas_call(
        _inproj_kernel,
        grid=(tokens // TM_PROJ,),
        in_specs=[row(D_MODEL), _full((1, D_MODEL)), _full((D_MODEL, D_IN_PROJ)),
                  _full((1, D_ATTN)), _full((1, D_KV)), tab_q, tab_q, tab_k, tab_k,
                  _full((D_ATTN, D_ATTN)), _full((D_KV, D_KV))],
        out_specs=[row(D_ATTN),
                   pl.BlockSpec((2 * N_KV_HEADS, TM_PROJ, D_KV), lambda i: (0, i, 0)),
                   pl.BlockSpec((None, N_KV_HEADS, D_KV, TM_PROJ),
                                lambda i: (i // n_seq_tiles, 0, 0, i % n_seq_tiles)),
                   row(5 * D_HGRN), row(2 * D_CONV)],
        out_shape=[jax.ShapeDtypeStruct((tokens, D_ATTN), BF16),
                   jax.ShapeDtypeStruct((2 * N_KV_HEADS, tokens, D_KV), BF16),
                   jax.ShapeDtypeStruct((batch, N_KV_HEADS, D_KV, seq), BF16),
                   jax.ShapeDtypeStruct((tokens, 5 * D_HGRN), F32),
                   jax.ShapeDtypeStruct((tokens, 2 * D_CONV), F32)],
        compiler_params=_params(1),
        name="inproj",
    )(x2, nw, w_in, qnw, knw, cq, sq, ck, sk, segq, segk)


def _attn_kernel(q_ref, kp_ref, va_ref, nw_ref, o_ref, st_ref, yt_ref):
    n_items = ATTN_SUBTILES * N_HEADS

    def scores_t(it):
        sub, hd = divmod(it, N_HEADS)
        g = hd // (N_HEADS // N_KV_HEADS)
        qp = q_ref[sub * TQ_ATTN:(sub + 1) * TQ_ATTN, (hd // 2) * LANES:(hd // 2 + 1) * LANES]
        st_ref[it % ATTN_SLOTS] = lax.dot_general(kp_ref[2 * g + hd % 2], qp, (((1,), (1,)), ((), ())),
                                                  preferred_element_type=F32)

    for it in range(ATTN_AHEAD):
        scores_t(it)
    for it in range(n_items):
        sub, hd = divmod(it, N_HEADS)
        g = hd // (N_HEADS // N_KV_HEADS)
        if it + ATTN_AHEAD < n_items:
            scores_t(it + ATTN_AHEAD)
        slot = it % ATTN_SLOTS
        m = jnp.max(st_ref[slot], axis=0, keepdims=True)
        p = jnp.exp2(st_ref[slot] - m).astype(BF16)
        r = jnp.dot(va_ref[g], p, preferred_element_type=F32)
        num = r[g * HEAD_DIM:(g + 1) * HEAD_DIM]
        den = r[(1 - g) * HEAD_DIM:(1 - g) * HEAD_DIM + 1]
        yt_ref[hd * HEAD_DIM:(hd + 1) * HEAD_DIM, :] = num / den
        if hd == N_HEADS - 1:
            y = yt_ref[...].T
            ms = jnp.mean(y * y, axis=-1, keepdims=True)
            o_ref[sub * TQ_ATTN:(sub + 1) * TQ_ATTN, :] = (y * lax.rsqrt(ms + EPS) * nw_ref[...]).astype(BF16)


def _attention(q, kp, va, nw, batch, seq):
    tokens = q.shape[0]
    rows = ATTN_SUBTILES * TQ_ATTN
    nq = seq // rows
    return pl.pallas_call(
        _attn_kernel,
        grid=(batch, nq),
        in_specs=[pl.BlockSpec((rows, D_ATTN), lambda b, j: (b * nq + j, 0)),
                  pl.BlockSpec((2 * N_KV_HEADS, seq, D_KV), lambda b, j: (0, b, 0)),
                  pl.BlockSpec((None, N_KV_HEADS, D_KV, seq), lambda b, j: (b, 0, 0, 0)),
                  pl.BlockSpec((1, D_ATTN), lambda b, j: (0, 0))],
        out_specs=pl.BlockSpec((rows, D_ATTN), lambda b, j: (b * nq + j, 0)),
        out_shape=jax.ShapeDtypeStruct((tokens, D_ATTN), BF16),
        scratch_shapes=[pltpu.VMEM((ATTN_SLOTS, seq, TQ_ATTN), F32), pltpu.VMEM((D_ATTN, TQ_ATTN), F32)],
        compiler_params=_params(2),
        name="attn",
    )(q, kp, va, nw)


HG_LEVELS = (32, 16, 8)
N_EXP_GROUPS = 2 + len(HG_LEVELS)


def _hgrn_constants():
    n = HG_CHUNK
    mats = []
    idx = np.arange(n)
    cum = (idx[None, :] <= idx[:, None]).astype(np.float32)
    mats.append(cum)
    for half in HG_LEVELS:
        m = np.zeros((n, n), np.float32)
        for p in range(n):
            if (p // half) % 2 == 1:
                p0 = (p // half) * half
                m[p, p0 + 1:p + 1] = 1.0
            else:
                p0 = (p // half) * half + half
                m[p, p + 1:p0 + 1] = 1.0
        mats.append(m)
    m = np.zeros((n, n), np.float32)
    for p in range(n):
        pr = (p // HG_FINE) * HG_FINE + HG_REF
        if p > pr:
            m[p, pr + 1:p + 1] = 1.0
        elif p < pr:
            m[p, p + 1:pr + 1] = -1.0
    mats.append(m)
    fwd = np.concatenate(mats, axis=0)
    bwd = np.concatenate([m[::-1, ::-1] for m in mats], axis=0)
    m2 = np.stack([np.tile(fwd, (1, 2)), np.tile(bwd, (1, 2))])

    lev = np.full((n, n), len(HG_LEVELS) + 1, np.int32)
    for t in range(n):
        for s in range(t + 1):
            if t // HG_FINE == s // HG_FINE:
                lev[t, s] = len(HG_LEVELS)
            else:
                for li, half in enumerate(HG_LEVELS):
                    if t // (2 * half) == s // (2 * half) and t // half != s // half:
                        lev[t, s] = li
                        break
    lev2 = np.stack([np.tile(lev, (1, HGRN_HEADS)), np.tile(lev[::-1, ::-1], (1, HGRN_HEADS))])
    head = np.arange(D_HGRN) // HEAD_DIM
    headmask = (head[:, None] == head[None, :]).astype(np.float32)
    return m2, lev2, headmask


def _hgrn_kernel(hq_ref, zf_ref, zb_ref, hi_ref, gate_ref, logit_ref, gnw_ref, m2_ref, lev_ref,
                 hmb_ref, seg_ref, o_ref, ofw_ref, obw_ref, st_ref, *, layer):
    seq = hq_ref.shape[0]
    n_chunks = seq // HG_CHUNK
    n = HG_CHUNK

    lg = [logit_ref[i] for i in range(DEPTH)]
    mx = functools.reduce(jnp.maximum, lg)
    ex = [jnp.exp(v - mx) for v in lg]
    den = functools.reduce(lambda a, b: a + b, ex)
    sm = [e / den for e in ex]
    acc = sm[0]
    for i in range(1, layer + 1):
        acc = acc + sm[i]
    lb = acc - sm[0]

    def stage_a(d, z_ref, lb_row, r0):
        z = z_ref[pl.ds(r0, n), :]
        f = lb_row + (1.0 - lb_row) * jax.nn.sigmoid(z)
        g = jnp.log(jnp.maximum(f, F_MIN)) * LOG2_E
        kk = 1.0 - f
        g1 = g.astype(BF16)
        g2 = (g - g1.astype(F32)).astype(BF16)
        gcat = jnp.concatenate([g1, g2], axis=0)
        expo = jnp.dot(m2_ref[d], gcat, preferred_element_type=F32)
        return kk, expo

    def stage_b(d, r0, kk, expo):
        hm_bf = hmb_ref[...]
        q = hq_ref[pl.ds(r0, n), :]
        v = hi_ref[pl.ds(r0, n), :]
        lev = lev_ref[d]
        a = jnp.zeros((n, D_HGRN), F32)
        for li in range(len(HG_LEVELS) + 1):
            e_lev = expo[(1 + li) * n:(2 + li) * n]
            x_q = jnp.exp2(e_lev)
            x_k = x_q if li < len(HG_LEVELS) else jnp.exp2(-e_lev)
            qt = (q * x_q).astype(BF16)
            kt = (kk * x_k).astype(BF16)
            kbd = jnp.concatenate([kt] * HGRN_HEADS, axis=0) * hm_bf
            sc = lax.dot_general(qt, kbd, (((1,), (1,)), ((), ())), preferred_element_type=F32)
            a = jnp.where(lev == li, sc, a)
        return a.astype(BF16)

    def stage_b2(d, r0, kk, expo, a):
        hm_bf = hmb_ref[...]
        q = hq_ref[pl.ds(r0, n), :]
        v = hi_ref[pl.ds(r0, n), :]
        vbd = jnp.concatenate([v.astype(BF16)] * HGRN_HEADS, axis=0) * hm_bf
        o_intra = jnp.dot(a, vbd, preferred_element_type=F32)

        e_cum = expo[0:n]
        e_tot = e_cum[n - 1:n] if d == 0 else e_cum[0:1]
        qb = (q * jnp.exp2(e_cum)).astype(BF16)
        kdec = (kk * jnp.exp2(e_tot - e_cum)).astype(BF16)
        zpad = jnp.zeros((LANES - n, D_HGRN), F32)
        vt = jnp.concatenate([v, zpad], axis=0).T.astype(BF16)
        kpad = jnp.concatenate([kdec, jnp.zeros((LANES - n, D_HGRN), BF16)], axis=0)
        upd = jnp.dot(vt, kpad, preferred_element_type=F32)
        return o_intra, qb, upd, jnp.exp2(e_tot)

    def stage_c(d, o_intra, qb, upd, decay_tot):
        st = st_ref[d]
        o = o_intra + lax.dot_general(qb, st.astype(BF16) * hmb_ref[...], (((1,), (1,)), ((), ())),
                                      preferred_element_type=F32)
        st_ref[d] = st * decay_tot + upd
        return o

    st_ref[...] = jnp.zeros_like(st_ref)

    def scan_body(ci, carry):
        items = []
        for u in range(HG_UNROLL):
            cf = ci * HG_UNROLL + u
            items.append((0, zf_ref, lb[0:1], pl.multiple_of(cf * n, n), ofw_ref))
            items.append((1, zb_ref, lb[1:2], pl.multiple_of((n_chunks - 1 - cf) * n, n), obw_ref))
        sa = [stage_a(d, z_ref, lb_row, r0) for d, z_ref, lb_row, r0, _ in items]
        sb = [stage_b(d, r0, *a) for (d, _, _, r0, _), a in zip(items, sa)]
        sb2 = [stage_b2(d, r0, *a, b) for (d, _, _, r0, _), a, b in zip(items, sa, sb)]
        for (d, _, _, r0, out_ref), b in zip(items, sb2):
            out_ref[pl.ds(r0, n), :] = stage_c(d, *b)
        return carry

    lax.fori_loop(0, n_chunks // HG_UNROLL, scan_body, 0)

    gnw = gnw_ref[...]

    def finish_body(i, carry):
        r0 = pl.multiple_of(i * HG_FINISH_ROWS, HG_FINISH_ROWS)
        o = ofw_ref[pl.ds(r0, HG_FINISH_ROWS), :] + obw_ref[pl.ds(r0, HG_FINISH_ROWS), :]
        on = o * lax.rsqrt(_segment_mean_sq(o, seg_ref, HEAD_DIM) + EPS) * gnw
        gate = gate_ref[pl.ds(r0, HG_FINISH_ROWS), :]
        o_ref[pl.ds(r0, HG_FINISH_ROWS), :] = (on * (gate * jax.nn.sigmoid(gate))).astype(BF16)
        return carry

    lax.fori_loop(0, seq // HG_FINISH_ROWS, finish_body, 0)


def _hgrn(hg, logits, gnw, consts, seg, layer, batch, seq):
    tokens = hg.shape[0]
    m2, lev, hmb = consts
    col = lambda j: pl.BlockSpec((seq, D_HGRN), lambda b: (b, j))
    return pl.pallas_call(
        functools.partial(_hgrn_kernel, layer=layer),
        grid=(batch,),
        in_specs=[col(0), col(1), col(2), col(3), col(4),
                  _full(logits.shape), _full((1, D_HGRN)), _full(m2.shape), _full(lev.shape),
                  _full(hmb.shape), _full(seg.shape)],
        out_specs=pl.BlockSpec((seq, D_HGRN), lambda b: (b, 0)),
        out_shape=jax.ShapeDtypeStruct((tokens, D_HGRN), BF16),
        scratch_shapes=[pltpu.VMEM((seq, D_HGRN), F32), pltpu.VMEM((seq, D_HGRN), F32),
                        pltpu.VMEM((2, D_HGRN, D_HGRN), F32)],
        compiler_params=_params(1),
        name="hgrn",
    )(hg, hg, hg, hg, hg, logits, gnw, m2, lev, hmb, seg)


def _conv_kernel(a_ref, b_ref, dww_ref, dwb_ref, lnw_ref, lnb_ref, pww_ref, pwb_ref, nw_ref,
                 o_ref, pad_ref, dw_ref):
    seq = a_ref.shape[0]
    halo = jnp.zeros((CONV_HALO, D_CONV), F32)
    pad_ref[0:CONV_HALO, :] = halo
    pad_ref[CONV_HALO + seq:CONV_HALO + seq + CONV_HALO, :] = halo
    pad_ref[CONV_HALO:CONV_HALO + seq, :] = a_ref[...] * jax.nn.sigmoid(b_ref[...])

    win_rows = CONV_ROWS + 2 * CONV_HALO

    def tap_body(i, carry):
        r0 = pl.multiple_of(i * CONV_ROWS, CONV_ROWS)
        win = pad_ref[pl.ds(r0, win_rows), :]
        acc = jnp.zeros((CONV_ROWS, D_CONV), F32)
        for r in range(SUBLANES):
            shifted = win if r == 0 else pltpu.roll(win, win_rows - r, 0)
            for a8 in range(0, 2 * CONV_HALO, SUBLANES):
                tap = a8 + r - (CONV_HALO - CONV_PAD)
                if 0 <= tap < CONV_WIDTH:
                    acc = acc + shifted[a8:a8 + CONV_ROWS] * dww_ref[tap:tap + 1, :]
        dw_ref[pl.ds(r0, CONV_ROWS), :] = acc + dwb_ref[...]
        return carry

    lax.fori_loop(0, seq // CONV_ROWS, tap_body, 0, unroll=2)

    def post_body(i, carry):
        r0 = pl.multiple_of(i * CONV_POST_ROWS, CONV_POST_ROWS)
        u = dw_ref[pl.ds(r0, CONV_POST_ROWS), :]
        mu = jnp.mean(u, axis=-1, keepdims=True)
        uc = u - mu
        var = jnp.mean(uc * uc, axis=-1, keepdims=True)
        y = uc * lax.rsqrt(var + LN_EPS) * lnw_ref[...] + lnb_ref[...]
        y = y * jax.nn.sigmoid(y)
        y = jnp.dot(y.astype(BF16), pww_ref[...], preferred_element_type=F32) + pwb_ref[...]
        ms = jnp.mean(y * y, axis=-1, keepdims=True)
        o_ref[pl.ds(r0, CONV_POST_ROWS), :] = (y * lax.rsqrt(ms + EPS) * nw_ref[...]).astype(BF16)
        return carry

    lax.fori_loop(0, seq // CONV_POST_ROWS, post_body, 0, unroll=2)


def _conv(cv, dww, dwb, lnw, lnb, pww, pwb, nw, batch, seq):
    tokens = cv.shape[0]
    col = lambda j: pl.BlockSpec((seq, D_CONV), lambda b: (b, j))
    vec = _full((1, D_CONV))
    return pl.pallas_call(
        _conv_kernel,
        grid=(batch,),
        in_specs=[col(0), col(1), _full(dww.shape), vec, vec, vec, _full((D_CONV, D_CONV)), vec, vec],
        out_specs=pl.BlockSpec((seq, D_CONV), lambda b: (b, 0)),
        out_shape=jax.ShapeDtypeStruct((tokens, D_CONV), BF16),
        scratch_shapes=[pltpu.VMEM((seq + 2 * CONV_HALO, D_CONV), F32), pltpu.VMEM((seq, D_CONV), F32)],
        compiler_params=_params(1),
        name="conv",
    )(cv, cv, dww, dwb, lnw, lnb, pww, pwb, nw)


def _ffn_kernel(x_ref, ya_ref, yh_ref, yc_ref, wo_ref, nw_ref, wg_ref, wu_ref, wd_ref,
                o_ref, act_ref):
    x1 = x_ref[...]
    x1 = x1 + jnp.dot(ya_ref[...], wo_ref[0:D_ATTN, :], preferred_element_type=F32)
    x1 = x1 + jnp.dot(yh_ref[...], wo_ref[D_ATTN:D_ATTN + D_HGRN, :], preferred_element_type=F32)
    x1 = x1 + jnp.dot(yc_ref[...], wo_ref[D_ATTN + D_HGRN:D_MODEL, :], preferred_element_type=F32)
    ms = jnp.mean(x1 * x1, axis=-1, keepdims=True)
    h = (x1 * lax.rsqrt(ms + EPS) * nw_ref[...]).astype(BF16)
    for c in range(0, D_FF, FF_CHUNK):
        g = jnp.dot(h, wg_ref[:, c:c + FF_CHUNK], preferred_element_type=F32)
        u = jnp.dot(h, wu_ref[:, c:c + FF_CHUNK], preferred_element_type=F32)
        act_ref[:, c:c + FF_CHUNK] = (g * jax.nn.sigmoid(g) * u).astype(BF16)
    o_ref[...] = x1 + jnp.dot(act_ref[...], wd_ref[...], preferred_element_type=F32)


def _ffn(x2, ya, yh, yc, wo, nw, wg, wu, wd):
    tokens = x2.shape[0]
    row = lambda w: pl.BlockSpec((TM_FFN, w), lambda i: (i, 0))
    resident = lambda shape: pl.BlockSpec(shape, lambda i: (0, 0), pipeline_mode=pl.Buffered(1))
    return pl.pallas_call(
        _ffn_kernel,
        grid=(tokens // TM_FFN,),
        in_specs=[row(D_MODEL), row(D_ATTN), row(D_HGRN), row(D_CONV),
                  resident((D_MODEL, D_MODEL)), _full((1, D_MODEL)),
                  resident((D_MODEL, D_FF)), resident((D_MODEL, D_FF)), resident((D_FF, D_MODEL))],
        out_specs=row(D_MODEL),
        out_shape=jax.ShapeDtypeStruct((tokens, D_MODEL), F32),
        scratch_shapes=[pltpu.VMEM((TM_FFN, D_FF), BF16)],
        compiler_params=_params(1),
        name="ffn",
    )(x2, ya, yh, yc, wo, nw, wg, wu, wd)


def _rope_tables(seq):
    rows = seq // GRID_W
    row_id = jnp.repeat(jnp.arange(rows, dtype=F32), GRID_W)
    col_id = jnp.tile(jnp.arange(GRID_W, dtype=F32), rows)
    half = HEAD_DIM // 2
    inv_freq = ROPE_THETA ** (-jnp.arange(0, half, 2, dtype=F32) / half)
    ang_r = row_id[:, None] * inv_freq[None, :]
    ang_c = col_id[:, None] * inv_freq[None, :]
    ang = jnp.concatenate([ang_r, ang_r, ang_c, ang_c], axis=-1)
    cos, sin = jnp.cos(ang), jnp.sin(ang)
    lane = np.arange(HEAD_DIM)
    sign = jnp.asarray(np.where((lane % (HEAD_DIM // 2)) < HEAD_DIM // 4, -1.0, 1.0), F32)
    sin_signed = sin * sign[None, :]
    scale = HEAD_DIM ** -0.5 * np.log2(np.e)
    return (jnp.tile(cos, (1, N_HEADS)) * scale, jnp.tile(sin_signed, (1, N_HEADS)) * scale,
            jnp.tile(cos, (1, N_KV_HEADS)), jnp.tile(sin_signed, (1, N_KV_HEADS)))


def _segment_matrix(width, seg_len):
    seg = np.arange(width) // seg_len
    return jnp.asarray((seg[:, None] == seg[None, :]).astype(np.float32), BF16)


def kernel(x, mix_norm_w, w_in, q_norm_w, k_norm_w, hgrn_lb_logits, hgrn_gnorm_w, conv_dw_w, conv_dw_b,
           conv_ln_w, conv_ln_b, conv_pw_w, conv_pw_b, attn_out_norm_w, conv_out_norm_w, w_out,
           ffn_norm_w, w_gate, w_up, w_down):
    batch, seq, d_model = x.shape
    assert d_model == D_MODEL and seq % TM_PROJ == 0 and seq % HG_CHUNK == 0
    assert seq % (ATTN_SUBTILES * TQ_ATTN) == 0 and seq % CONV_POST_ROWS == 0 and seq % HG_FINISH_ROWS == 0
    assert (batch * seq) % TM_FFN == 0 and w_in.shape == (DEPTH, D_MODEL, D_IN_PROJ)
    assert w_gate.shape == (DEPTH, D_MODEL, D_FF)

    tabs = _rope_tables(seq)
    segq = _segment_matrix(D_ATTN, HEAD_DIM)
    segk = _segment_matrix(D_KV, HEAD_DIM)
    segh = _segment_matrix(D_HGRN, HEAD_DIM)
    m2, lev, hm = _hgrn_constants()
    hconsts = (jnp.asarray(m2, BF16), jnp.asarray(lev), jnp.asarray(hm, BF16))
    logits = hgrn_lb_logits.astype(F32)

    x2 = x.reshape(batch * seq, D_MODEL)
    for l in range(DEPTH):
        q, kp, va, hg, cv = _inproj(
            x2, mix_norm_w[l][None, :], w_in[l].astype(BF16),
            jnp.tile(q_norm_w[l], N_HEADS)[None, :], jnp.tile(k_norm_w[l], N_KV_HEADS)[None, :],
            tabs, segq, segk, batch, seq)
        ya = _attention(q, kp, va, attn_out_norm_w[l][None, :], batch, seq)
        yh = _hgrn(hg, logits, jnp.tile(hgrn_gnorm_w[l], HGRN_HEADS)[None, :], hconsts, segh, l, batch, seq)
        yc = _conv(cv, conv_dw_w[l], conv_dw_b[l][None, :], conv_ln_w[l][None, :], conv_ln_b[l][None, :],
                   conv_pw_w[l].astype(BF16), conv_pw_b[l][None, :], conv_out_norm_w[l][None, :], batch, seq)
        x2 = _ffn(x2, ya, yh, yc, w_out[l].astype(BF16), ffn_norm_w[l][None, :],
                  w_gate[l].astype(BF16), w_up[l].astype(BF16), w_down[l].astype(BF16))
    return x2.reshape(batch, seq, D_MODEL)
```

```python
import functools

import numpy as np
import jax
import jax.numpy as jnp
from jax import lax
from jax.experimental import pallas as pl
from jax.experimental.pallas import tpu as pltpu

F32 = jnp.float32
BF16 = jnp.bfloat16

D_MODEL = 1024
DEPTH = 2
GRID_W = 64
D_ATTN = 512
D_HGRN = 256
D_CONV = 256
HEAD_DIM = 64
N_HEADS = D_ATTN // HEAD_DIM
N_KV_HEADS = 2
D_KV = N_KV_HEADS * HEAD_DIM
ROPE_THETA = 10000.0
HGRN_HEADS = D_HGRN // HEAD_DIM
F_MIN = 1e-6
LOG2_E = float(np.log2(np.e))
CONV_WIDTH = 31
CONV_PAD = (CONV_WIDTH - 1) // 2
D_FF = 2816
EPS = 1e-6
LN_EPS = 1e-5
D_IN_PROJ = D_ATTN + 2 * D_KV + 5 * D_HGRN + 2 * D_CONV

OFF_Q = 0
OFF_K = OFF_Q + D_ATTN
OFF_V = OFF_K + D_KV
OFF_H = OFF_V + D_KV
OFF_C = OFF_H + 5 * D_HGRN

LANES = 128
SUBLANES = 8
VMEM_LIMIT_BYTES = 56 * 1024 * 1024

TM_PROJ = 512
TQ_ATTN = 256
ATTN_SUBTILES = 4
ATTN_AHEAD = 3
ATTN_SLOTS = ATTN_AHEAD + 1
TM_FFN = 512
FF_CHUNK = 256
HG_CHUNK = 64
HG_FINE = 8
HG_REF = 2
HG_FINISH_ROWS = 256
HG_UNROLL = 4
CONV_ROWS = 64
CONV_POST_ROWS = 256
CONV_HALO = 16

assert -np.log(F_MIN) * (HG_FINE - 1 - HG_REF) < 80.0 and -np.log(F_MIN) * HG_REF < 80.0


def _params(n_grid_axes):
    return pltpu.CompilerParams(
        dimension_semantics=("arbitrary",) * n_grid_axes,
        vmem_limit_bytes=VMEM_LIMIT_BYTES)


def _full(shape):
    nd = len(shape)
    return pl.BlockSpec(shape, lambda *_: (0,) * nd)


def _segment_mean_sq(y, seg_ref, seg_len):
    return jnp.dot((y * y).astype(BF16), seg_ref[...], preferred_element_type=F32) * (1.0 / seg_len)


def _rope(xn, cos, sin_signed, first_mask):
    width = xn.shape[1]
    left = pltpu.roll(xn, width - HEAD_DIM // 4, 1)
    right = pltpu.roll(xn, HEAD_DIM // 4, 1)
    rot = jnp.where(first_mask, left, right)
    return xn * cos + rot * sin_signed


def _inproj_kernel(x_ref, nw_ref, w_ref, qnw_ref, knw_ref, cq_ref, sq_ref, ck_ref, sk_ref,
                   segq_ref, segk_ref,
                   q_out, kp_out, va_out, hg_out, cv_out):
    x = x_ref[...]
    ms = jnp.mean(x * x, axis=-1, keepdims=True)
    h = (x * lax.rsqrt(ms + EPS) * nw_ref[...]).astype(BF16)

    q = jnp.dot(h, w_ref[:, OFF_Q:OFF_K], preferred_element_type=F32)
    qn = q * lax.rsqrt(_segment_mean_sq(q, segq_ref, HEAD_DIM) + EPS) * qnw_ref[...]
    lane_q = lax.broadcasted_iota(jnp.int32, (1, D_ATTN), 1)
    first_q = (lane_q & (HEAD_DIM // 2 - 1)) < HEAD_DIM // 4
    q_out[...] = _rope(qn, cq_ref[...], sq_ref[...], first_q).astype(BF16)

    k = jnp.dot(h, w_ref[:, OFF_K:OFF_V], preferred_element_type=F32)
    kn = k * lax.rsqrt(_segment_mean_sq(k, segk_ref, HEAD_DIM) + EPS) * knw_ref[...]
    lane_k = lax.broadcasted_iota(jnp.int32, (1, D_KV), 1)
    first_k = (lane_k & (HEAD_DIM // 2 - 1)) < HEAD_DIM // 4
    kr = _rope(kn, ck_ref[...], sk_ref[...], first_k)
    ks = pltpu.roll(kr, HEAD_DIM, 1)
    low = lane_k < HEAD_DIM
    zero = jnp.zeros_like(kr)
    kp_out[0] = jnp.where(low, kr, zero).astype(BF16)
    kp_out[1] = jnp.where(low, zero, ks).astype(BF16)
    kp_out[2] = jnp.where(low, ks, zero).astype(BF16)
    kp_out[3] = jnp.where(low, zero, kr).astype(BF16)

    v = jnp.dot(h, w_ref[:, OFF_V:OFF_H], preferred_element_type=F32)
    vt = v.T
    ones = jnp.ones((HEAD_DIM, vt.shape[1]), F32)
    va_out[0] = jnp.concatenate([vt[0:HEAD_DIM], ones], axis=0).astype(BF16)
    va_out[1] = jnp.concatenate([ones, vt[HEAD_DIM:D_KV]], axis=0).astype(BF16)

    hg_out[...] = jnp.dot(h, w_ref[:, OFF_H:OFF_C], preferred_element_type=F32)
    cv_out[...] = jnp.dot(h, w_ref[:, OFF_C:D_IN_PROJ], preferred_element_type=F32)


def _inproj(x2, nw, w_in, qnw, knw, tabs, segq, segk, batch, seq):
    tokens = x2.shape[0]
    n_seq_tiles = seq // TM_PROJ
    cq, sq, ck, sk = tabs
    tab_q = pl.BlockSpec((TM_PROJ, D_ATTN), lambda i: (i % n_seq_tiles, 0))
    tab_k = pl.BlockSpec((TM_PROJ, D_KV), lambda i: (i % n_seq_tiles, 0))
    row = lambda w: pl.BlockSpec((TM_PROJ, w), lambda i: (i, 0))
    return pl.pallas_call(
        _inproj_kernel,
        grid=(tokens // TM_PROJ,),
        in_specs=[row(D_MODEL), _full((1, D_MODEL)), _full((D_MODEL, D_IN_PROJ)),
                  _full((1, D_ATTN)), _full((1, D_KV)), tab_q, tab_q, tab_k, tab_k,
                  _full((D_ATTN, D_ATTN)), _full((D_KV, D_KV))],
        out_specs=[row(D_ATTN),
                   pl.BlockSpec((2 * N_KV_HEADS, TM_PROJ, D_KV), lambda i: (0, i, 0)),
                   pl.BlockSpec((None, N_KV_HEADS, D_KV, TM_PROJ),
                                lambda i: (i // n_seq_tiles, 0, 0, i % n_seq_tiles)),
                   row(5 * D_HGRN), row(2 * D_CONV)],
        out_shape=[jax.ShapeDtypeStruct((tokens, D_ATTN), BF16),
                   jax.ShapeDtypeStruct((2 * N_KV_HEADS, tokens, D_KV), BF16),
                   jax.ShapeDtypeStruct((batch, N_KV_HEADS, D_KV, seq), BF16),
                   jax.ShapeDtypeStruct((tokens, 5 * D_HGRN), F32),
                   jax.ShapeDtypeStruct((tokens, 2 * D_CONV), F32)],
        compiler_params=_params(1),
        name="inproj",
    )(x2, nw, w_in, qnw, knw, cq, sq, ck, sk, segq, segk)


def _attn_kernel(q_ref, kp_ref, va_ref, nw_ref, o_ref, st_ref, yt_ref):
    n_items = ATTN_SUBTILES * N_HEADS

    def scores_t(it):
        sub, hd = divmod(it, N_HEADS)
        g = hd // (N_HEADS // N_KV_HEADS)
        qp = q_ref[sub * TQ_ATTN:(sub + 1) * TQ_ATTN, (hd // 2) * LANES:(hd // 2 + 1) * LANES]
        st_ref[it % ATTN_SLOTS] = lax.dot_general(kp_ref[2 * g + hd % 2], qp, (((1,), (1,)), ((), ())),
                                                  preferred_element_type=F32)

    for it in range(ATTN_AHEAD):
        scores_t(it)
    for it in range(n_items):
        sub, hd = divmod(it, N_HEADS)
        g = hd // (N_HEADS // N_KV_HEADS)
        if it + ATTN_AHEAD < n_items:
            scores_t(it + ATTN_AHEAD)
        slot = it % ATTN_SLOTS
        m = jnp.max(st_ref[slot], axis=0, keepdims=True)
        p = jnp.exp2((st_ref[slot] - m).astype(BF16))
        r = jnp.dot(va_ref[g], p, preferred_element_type=F32)
        num = r[g * HEAD_DIM:(g + 1) * HEAD_DIM]
        den = r[(1 - g) * HEAD_DIM:(1 - g) * HEAD_DIM + 1]
        yt_ref[hd * HEAD_DIM:(hd + 1) * HEAD_DIM, :] = num / den
        if hd == N_HEADS - 1:
            y = yt_ref[...].T
            ms = jnp.mean(y * y, axis=-1, keepdims=True)
            o_ref[sub * TQ_ATTN:(sub + 1) * TQ_ATTN, :] = (y * lax.rsqrt(ms + EPS) * nw_ref[...]).astype(BF16)


def _attention(q, kp, va, nw, batch, seq):
    tokens = q.shape[0]
    rows = ATTN_SUBTILES * TQ_ATTN
    nq = seq // rows
    return pl.pallas_call(
        _attn_kernel,
        grid=(batch, nq),
        in_specs=[pl.BlockSpec((rows, D_ATTN), lambda b, j: (b * nq + j, 0)),
                  pl.BlockSpec((2 * N_KV_HEADS, seq, D_KV), lambda b, j: (0, b, 0)),
                  pl.BlockSpec((None, N_KV_HEADS, D_KV, seq), lambda b, j: (b, 0, 0, 0)),
                  pl.BlockSpec((1, D_ATTN), lambda b, j: (0, 0))],
        out_specs=pl.BlockSpec((rows, D_ATTN), lambda b, j: (b * nq + j, 0)),
        out_shape=jax.ShapeDtypeStruct((tokens, D_ATTN), BF16),
        scratch_shapes=[pltpu.VMEM((ATTN_SLOTS, seq, TQ_ATTN), F32), pltpu.VMEM((D_ATTN, TQ_ATTN), F32)],
        compiler_params=_params(2),
        name="attn",
    )(q, kp, va, nw)


HG_LEVELS = (32, 16, 8)
N_EXP_GROUPS = 2 + len(HG_LEVELS)


def _hgrn_constants():
    n = HG_CHUNK
    mats = []
    idx = np.arange(n)
    cum = (idx[None, :] <= idx[:, None]).astype(np.float32)
    mats.append(cum)
    for half in HG_LEVELS:
        m = np.zeros((n, n), np.float32)
        for p in range(n):
            if (p // half) % 2 == 1:
                p0 = (p // half) * half
                m[p, p0 + 1:p + 1] = 1.0
            else:
                p0 = (p // half) * half + half
                m[p, p + 1:p0 + 1] = 1.0
        mats.append(m)
    m = np.zeros((n, n), np.float32)
    for p in range(n):
        pr = (p // HG_FINE) * HG_FINE + HG_REF
        if p > pr:
            m[p, pr + 1:p + 1] = 1.0
        elif p < pr:
            m[p, p + 1:pr + 1] = -1.0
    mats.append(m)
    fwd = np.concatenate(mats, axis=0)
    bwd = np.concatenate([m[::-1, ::-1] for m in mats], axis=0)
    m2 = np.stack([np.tile(fwd, (1, 2)), np.tile(bwd, (1, 2))])

    lev = np.full((n, n), len(HG_LEVELS) + 1, np.int32)
    for t in range(n):
        for s in range(t + 1):
            if t // HG_FINE == s // HG_FINE:
                lev[t, s] = len(HG_LEVELS)
            else:
                for li, half in enumerate(HG_LEVELS):
                    if t // (2 * half) == s // (2 * half) and t // half != s // half:
                        lev[t, s] = li
                        break
    lev2 = np.stack([np.tile(lev, (1, HGRN_HEADS)), np.tile(lev[::-1, ::-1], (1, HGRN_HEADS))])
    head = np.arange(D_HGRN) // HEAD_DIM
    headmask = (head[:, None] == head[None, :]).astype(np.float32)
    return m2, lev2, headmask


def _hgrn_kernel(hq_ref, zf_ref, zb_ref, hi_ref, gate_ref, logit_ref, gnw_ref, m2_ref, lev_ref,
                 hmb_ref, seg_ref, o_ref, ofw_ref, obw_ref, st_ref, *, layer):
    seq = hq_ref.shape[0]
    n_chunks = seq // HG_CHUNK
    n = HG_CHUNK

    lg = [logit_ref[i] for i in range(DEPTH)]
    mx = functools.reduce(jnp.maximum, lg)
    ex = [jnp.exp(v - mx) for v in lg]
    den = functools.reduce(lambda a, b: a + b, ex)
    sm = [e / den for e in ex]
    acc = sm[0]
    for i in range(1, layer + 1):
        acc = acc + sm[i]
    lb = acc - sm[0]

    def stage_a(d, z_ref, lb_row, r0):
        z = z_ref[pl.ds(r0, n), :]
        f = lb_row + (1.0 - lb_row) * jax.nn.sigmoid(z)
        g = jnp.log(jnp.maximum(f, F_MIN)) * LOG2_E
        kk = 1.0 - f
        g1 = g.astype(BF16)
        g2 = (g - g1.astype(F32)).astype(BF16)
        gcat = jnp.concatenate([g1, g2], axis=0)
        expo = jnp.dot(m2_ref[d], gcat, preferred_element_type=F32)
        return kk, expo

    def stage_b(d, r0, kk, expo):
        hm_bf = hmb_ref[...]
        q = hq_ref[pl.ds(r0, n), :]
        v = hi_ref[pl.ds(r0, n), :]
        lev = lev_ref[d]
        a = jnp.zeros((n, D_HGRN), F32)
        for li in range(len(HG_LEVELS) + 1):
            e_lev = expo[(1 + li) * n:(2 + li) * n]
            x_q = jnp.exp2(e_lev)
            x_k = x_q if li < len(HG_LEVELS) else jnp.exp2(-e_lev)
            qt = (q * x_q).astype(BF16)
            kt = (kk * x_k).astype(BF16)
            kbd = jnp.concatenate([kt] * HGRN_HEADS, axis=0) * hm_bf
            sc = lax.dot_general(qt, kbd, (((1,), (1,)), ((), ())), preferred_element_type=F32)
            a = jnp.where(lev == li, sc, a)
        return a.astype(BF16)

    def stage_b2(d, r0, kk, expo, a):
        hm_bf = hmb_ref[...]
        q = hq_ref[pl.ds(r0, n), :]
        v = hi_ref[pl.ds(r0, n), :]
        vbd = jnp.concatenate([v.astype(BF16)] * HGRN_HEADS, axis=0) * hm_bf
        o_intra = jnp.dot(a, vbd, preferred_element_type=F32)

        e_cum = expo[0:n]
        e_tot = e_cum[n - 1:n] if d == 0 else e_cum[0:1]
        qb = (q * jnp.exp2(e_cum)).astype(BF16)
        kdec = (kk * jnp.exp2(e_tot - e_cum)).astype(BF16)
        zpad = jnp.zeros((LANES - n, D_HGRN), F32)
        vt = jnp.concatenate([v, zpad], axis=0).T.astype(BF16)
        kpad = jnp.concatenate([kdec, jnp.zeros((LANES - n, D_HGRN), BF16)], axis=0)
        upd = jnp.dot(vt, kpad, preferred_element_type=F32)
        return o_intra, qb, upd, jnp.exp2(e_tot)

    def stage_c(d, o_intra, qb, upd, decay_tot):
        st = st_ref[d]
        o = o_intra + lax.dot_general(qb, st.astype(BF16) * hmb_ref[...], (((1,), (1,)), ((), ())),
                                      preferred_element_type=F32)
        st_ref[d] = st * decay_tot + upd
        return o

    st_ref[...] = jnp.zeros_like(st_ref)

    def scan_body(ci, carry):
        items = []
        for u in range(HG_UNROLL):
            cf = ci * HG_UNROLL + u
            items.append((0, zf_ref, lb[0:1], pl.multiple_of(cf * n, n), ofw_ref))
            items.append((1, zb_ref, lb[1:2], pl.multiple_of((n_chunks - 1 - cf) * n, n), obw_ref))
        sa = [stage_a(d, z_ref, lb_row, r0) for d, z_ref, lb_row, r0, _ in items]
        sb = [stage_b(d, r0, *a) for (d, _, _, r0, _), a in zip(items, sa)]
        sb2 = [stage_b2(d, r0, *a, b) for (d, _, _, r0, _), a, b in zip(items, sa, sb)]
        for (d, _, _, r0, out_ref), b in zip(items, sb2):
            out_ref[pl.ds(r0, n), :] = stage_c(d, *b)
        return carry

    lax.fori_loop(0, n_chunks // HG_UNROLL, scan_body, 0)

    gnw = gnw_ref[...]

    def finish_body(i, carry):
        r0 = pl.multiple_of(i * HG_FINISH_ROWS, HG_FINISH_ROWS)
        o = ofw_ref[pl.ds(r0, HG_FINISH_ROWS), :] + obw_ref[pl.ds(r0, HG_FINISH_ROWS), :]
        on = o * lax.rsqrt(_segment_mean_sq(o, seg_ref, HEAD_DIM) + EPS) * gnw
        gate = gate_ref[pl.ds(r0, HG_FINISH_ROWS), :]
        o_ref[pl.ds(r0, HG_FINISH_ROWS), :] = (on * (gate * jax.nn.sigmoid(gate))).astype(BF16)
        return carry

    lax.fori_loop(0, seq // HG_FINISH_ROWS, finish_body, 0)


def _hgrn(hg, logits, gnw, consts, seg, layer, batch, seq):
    tokens = hg.shape[0]
    m2, lev, hmb = consts
    col = lambda j: pl.BlockSpec((seq, D_HGRN), lambda b: (b, j))
    return pl.pallas_call(
        functools.partial(_hgrn_kernel, layer=layer),
        grid=(batch,),
        in_specs=[col(0), col(1), col(2), col(3), col(4),
                  _full(logits.shape), _full((1, D_HGRN)), _full(m2.shape), _full(lev.shape),
                  _full(hmb.shape), _full(seg.shape)],
        out_specs=pl.BlockSpec((seq, D_HGRN), lambda b: (b, 0)),
        out_shape=jax.ShapeDtypeStruct((tokens, D_HGRN), BF16),
        scratch_shapes=[pltpu.VMEM((seq, D_HGRN), F32), pltpu.VMEM((seq, D_HGRN), F32),
                        pltpu.VMEM((2, D_HGRN, D_HGRN), F32)],
        compiler_params=_params(1),
        name="hgrn",
    )(hg, hg, hg, hg, hg, logits, gnw, m2, lev, hmb, seg)


def _conv_kernel(a_ref, b_ref, dww_ref, dwb_ref, lnw_ref, lnb_ref, pww_ref, pwb_ref, nw_ref,
                 o_ref, pad_ref, dw_ref):
    seq = a_ref.shape[0]
    halo = jnp.zeros((CONV_HALO, D_CONV), F32)
    pad_ref[0:CONV_HALO, :] = halo
    pad_ref[CONV_HALO + seq:CONV_HALO + seq + CONV_HALO, :] = halo
    pad_ref[CONV_HALO:CONV_HALO + seq, :] = a_ref[...] * jax.nn.sigmoid(b_ref[...])

    win_rows = CONV_ROWS + 2 * CONV_HALO

    def tap_body(i, carry):
        r0 = pl.multiple_of(i * CONV_ROWS, CONV_ROWS)
        win = pad_ref[pl.ds(r0, win_rows), :]
        acc = jnp.zeros((CONV_ROWS, D_CONV), F32)
        for r in range(SUBLANES):
            shifted = win if r == 0 else pltpu.roll(win, win_rows - r, 0)
            for a8 in range(0, 2 * CONV_HALO, SUBLANES):
                tap = a8 + r - (CONV_HALO - CONV_PAD)
                if 0 <= tap < CONV_WIDTH:
                    acc = acc + shifted[a8:a8 + CONV_ROWS] * dww_ref[tap:tap + 1, :]
        dw_ref[pl.ds(r0, CONV_ROWS), :] = acc + dwb_ref[...]
        return carry

    lax.fori_loop(0, seq // CONV_ROWS, tap_body, 0, unroll=2)

    def post_body(i, carry):
        r0 = pl.multiple_of(i * CONV_POST_ROWS, CONV_POST_ROWS)
        u = dw_ref[pl.ds(r0, CONV_POST_ROWS), :]
        mu = jnp.mean(u, axis=-1, keepdims=True)
        uc = u - mu
        var = jnp.mean(uc * uc, axis=-1, keepdims=True)
        y = uc * lax.rsqrt(var + LN_EPS) * lnw_ref[...] + lnb_ref[...]
        y = y * jax.nn.sigmoid(y)
        y = jnp.dot(y.astype(BF16), pww_ref[...], preferred_element_type=F32) + pwb_ref[...]
        ms = jnp.mean(y * y, axis=-1, keepdims=True)
        o_ref[pl.ds(r0, CONV_POST_ROWS), :] = (y * lax.rsqrt(ms + EPS) * nw_ref[...]).astype(BF16)
        return carry

    lax.fori_loop(0, seq // CONV_POST_ROWS, post_body, 0, unroll=2)


def _conv(cv, dww, dwb, lnw, lnb, pww, pwb, nw, batch, seq):
    tokens = cv.shape[0]
    col = lambda j: pl.BlockSpec((seq, D_CONV), lambda b: (b, j))
    vec = _full((1, D_CONV))
    return pl.pallas_call(
        _conv_kernel,
        grid=(batch,),
        in_specs=[col(0), col(1), _full(dww.shape), vec, vec, vec, _full((D_CONV, D_CONV)), vec, vec],
        out_specs=pl.BlockSpec((seq, D_CONV), lambda b: (b, 0)),
        out_shape=jax.ShapeDtypeStruct((tokens, D_CONV), BF16),
        scratch_shapes=[pltpu.VMEM((seq + 2 * CONV_HALO, D_CONV), F32), pltpu.VMEM((seq, D_CONV), F32)],
        compiler_params=_params(1),
        name="conv",
    )(cv, cv, dww, dwb, lnw, lnb, pww, pwb, nw)


def _ffn_kernel(x_ref, ya_ref, yh_ref, yc_ref, wo_ref, nw_ref, wg_ref, wu_ref, wd_ref,
                o_ref, act_ref):
    x1 = x_ref[...]
    x1 = x1 + jnp.dot(ya_ref[...], wo_ref[0:D_ATTN, :], preferred_element_type=F32)
    x1 = x1 + jnp.dot(yh_ref[...], wo_ref[D_ATTN:D_ATTN + D_HGRN, :], preferred_element_type=F32)
    x1 = x1 + jnp.dot(yc_ref[...], wo_ref[D_ATTN + D_HGRN:D_MODEL, :], preferred_element_type=F32)
    ms = jnp.mean(x1 * x1, axis=-1, keepdims=True)
    h = (x1 * lax.rsqrt(ms + EPS) * nw_ref[...]).astype(BF16)
    for c in range(0, D_FF, FF_CHUNK):
        g = jnp.dot(h, wg_ref[:, c:c + FF_CHUNK], preferred_element_type=F32)
        u = jnp.dot(h, wu_ref[:, c:c + FF_CHUNK], preferred_element_type=F32)
        act_ref[:, c:c + FF_CHUNK] = (g * jax.nn.sigmoid(g) * u).astype(BF16)
    o_ref[...] = x1 + jnp.dot(act_ref[...], wd_ref[...], preferred_element_type=F32)


def _ffn(x2, ya, yh, yc, wo, nw, wg, wu, wd):
    tokens = x2.shape[0]
    row = lambda w: pl.BlockSpec((TM_FFN, w), lambda i: (i, 0))
    resident = lambda shape: pl.BlockSpec(shape, lambda i: (0, 0), pipeline_mode=pl.Buffered(1))
    return pl.pallas_call(
        _ffn_kernel,
        grid=(tokens // TM_FFN,),
        in_specs=[row(D_MODEL), row(D_ATTN), row(D_HGRN), row(D_CONV),
                  resident((D_MODEL, D_MODEL)), _full((1, D_MODEL)),
                  resident((D_MODEL, D_FF)), resident((D_MODEL, D_FF)), resident((D_FF, D_MODEL))],
        out_specs=row(D_MODEL),
        out_shape=jax.ShapeDtypeStruct((tokens, D_MODEL), F32),
        scratch_shapes=[pltpu.VMEM((TM_FFN, D_FF), BF16)],
        compiler_params=_params(1),
        name="ffn",
    )(x2, ya, yh, yc, wo, nw, wg, wu, wd)


def _rope_tables(seq):
    rows = seq // GRID_W
    row_id = jnp.repeat(jnp.arange(rows, dtype=F32), GRID_W)
    col_id = jnp.tile(jnp.arange(GRID_W, dtype=F32), rows)
    half = HEAD_DIM // 2
    inv_freq = ROPE_THETA ** (-jnp.arange(0, half, 2, dtype=F32) / half)
    ang_r = row_id[:, None] * inv_freq[None, :]
    ang_c = col_id[:, None] * inv_freq[None, :]
    ang = jnp.concatenate([ang_r, ang_r, ang_c, ang_c], axis=-1)
    cos, sin = jnp.cos(ang), jnp.sin(ang)
    lane = np.arange(HEAD_DIM)
    sign = jnp.asarray(np.where((lane % (HEAD_DIM // 2)) < HEAD_DIM // 4, -1.0, 1.0), F32)
    sin_signed = sin * sign[None, :]
    scale = HEAD_DIM ** -0.5 * np.log2(np.e)
    return (jnp.tile(cos, (1, N_HEADS)) * scale, jnp.tile(sin_signed, (1, N_HEADS)) * scale,
            jnp.tile(cos, (1, N_KV_HEADS)), jnp.tile(sin_signed, (1, N_KV_HEADS)))


def _segment_matrix(width, seg_len):
    seg = np.arange(width) // seg_len
    return jnp.asarray((seg[:, None] == seg[None, :]).astype(np.float32), BF16)


def kernel(x, mix_norm_w, w_in, q_norm_w, k_norm_w, hgrn_lb_logits, hgrn_gnorm_w, conv_dw_w, conv_dw_b,
           conv_ln_w, conv_ln_b, conv_pw_w, conv_pw_b, attn_out_norm_w, conv_out_norm_w, w_out,
           ffn_norm_w, w_gate, w_up, w_down):
    batch, seq, d_model = x.shape
    assert d_model == D_MODEL and seq % TM_PROJ == 0 and seq % HG_CHUNK == 0
    assert seq % (ATTN_SUBTILES * TQ_ATTN) == 0 and seq % CONV_POST_ROWS == 0 and seq % HG_FINISH_ROWS == 0
    assert (batch * seq) % TM_FFN == 0 and w_in.shape == (DEPTH, D_MODEL, D_IN_PROJ)
    assert w_gate.shape == (DEPTH, D_MODEL, D_FF)

    tabs = _rope_tables(seq)
    segq = _segment_matrix(D_ATTN, HEAD_DIM)
    segk = _segment_matrix(D_KV, HEAD_DIM)
    segh = _segment_matrix(D_HGRN, HEAD_DIM)
    m2, lev, hm = _hgrn_constants()
    hconsts = (jnp.asarray(m2, BF16), jnp.asarray(lev), jnp.asarray(hm, BF16))
    logits = hgrn_lb_logits.astype(F32)

    x2 = x.reshape(batch * seq, D_MODEL)
    for l in range(DEPTH):
        q, kp, va, hg, cv = _inproj(
            x2, mix_norm_w[l][None, :], w_in[l].astype(BF16),
            jnp.tile(q_norm_w[l], N_HEADS)[None, :], jnp.tile(k_norm_w[l], N_KV_HEADS)[None, :],
            tabs, segq, segk, batch, seq)
        ya = _attention(q, kp, va, attn_out_norm_w[l][None, :], batch, seq)
        yh = _hgrn(hg, logits, jnp.tile(hgrn_gnorm_w[l], HGRN_HEADS)[None, :], hconsts, segh, l, batch, seq)
        yc = _conv(cv, conv_dw_w[l], conv_dw_b[l][None, :], conv_ln_w[l][None, :], conv_ln_b[l][None, :],
                   conv_pw_w[l].astype(BF16), conv_pw_b[l][None, :], conv_out_norm_w[l][None, :], batch, seq)
        x2 = _ffn(x2, ya, yh, yc, w_out[l].astype(BF16), ffn_norm_w[l][None, :],
                  w_gate[l].astype(BF16), w_up[l].astype(BF16), w_down[l].astype(BF16))
    return x2.reshape(batch, seq, D_MODEL)
```

```python
import functools

import numpy as np
import jax
import jax.numpy as jnp
from jax import lax
from jax.experimental import pallas as pl
from jax.experimental.pallas import tpu as pltpu

F32 = jnp.float32
BF16 = jnp.bfloat16

D_MODEL = 1024
DEPTH = 2
GRID_W = 64
D_ATTN = 512
D_HGRN = 256
D_CONV = 256
HEAD_DIM = 64
N_HEADS = D_ATTN // HEAD_DIM
N_KV_HEADS = 2
D_KV = N_KV_HEADS * HEAD_DIM
ROPE_THETA = 10000.0
HGRN_HEADS = D_HGRN // HEAD_DIM
F_MIN = 1e-6
LOG2_E = float(np.log2(np.e))
CONV_WIDTH = 31
CONV_PAD = (CONV_WIDTH - 1) // 2
D_FF = 2816
EPS = 1e-6
LN_EPS = 1e-5
D_IN_PROJ = D_ATTN + 2 * D_KV + 5 * D_HGRN + 2 * D_CONV

OFF_Q = 0
OFF_K = OFF_Q + D_ATTN
OFF_V = OFF_K + D_KV
OFF_H = OFF_V + D_KV
OFF_C = OFF_H + 5 * D_HGRN

LANES = 128
SUBLANES = 8
VMEM_LIMIT_BYTES = 56 * 1024 * 1024

TM_PROJ = 512
TQ_ATTN = 256
ATTN_SUBTILES = 4
ATTN_AHEAD = 3
ATTN_SLOTS = ATTN_AHEAD + 1
TM_FFN = 512
FF_CHUNK = 256
HG_CHUNK = 64
HG_FINE = 8
HG_REF = 2
HG_FINISH_ROWS = 256
HG_UNROLL = 4
CONV_ROWS = 64
CONV_POST_ROWS = 256
CONV_HALO = 16

assert -np.log(F_MIN) * (HG_FINE - 1 - HG_REF) < 80.0 and -np.log(F_MIN) * HG_REF < 80.0


def _params(n_grid_axes):
    return pltpu.CompilerParams(
        dimension_semantics=("arbitrary",) * n_grid_axes,
        vmem_limit_bytes=VMEM_LIMIT_BYTES)


def _full(shape):
    nd = len(shape)
    return pl.BlockSpec(shape, lambda *_: (0,) * nd)


def _segment_mean_sq(y, seg_ref, seg_len):
    return jnp.dot((y * y).astype(BF16), seg_ref[...], preferred_element_type=F32) * (1.0 / seg_len)


def _rope(xn, cos, sin_signed, first_mask):
    width = xn.shape[1]
    left = pltpu.roll(xn, width - HEAD_DIM // 4, 1)
    right = pltpu.roll(xn, HEAD_DIM // 4, 1)
    rot = jnp.where(first_mask, left, right)
    return xn * cos + rot * sin_signed


def _inproj_kernel(x_ref, nw_ref, w_ref, qnw_ref, knw_ref, cq_ref, sq_ref, ck_ref, sk_ref,
                   segq_ref, segk_ref,
                   q_out, kp_out, va_out, hg_out, cv_out):
    x = x_ref[...]
    ms = jnp.mean(x * x, axis=-1, keepdims=True)
    h = (x * lax.rsqrt(ms + EPS) * nw_ref[...]).astype(BF16)

    q = jnp.dot(h, w_ref[:, OFF_Q:OFF_K], preferred_element_type=F32)
    qn = q * lax.rsqrt(_segment_mean_sq(q, segq_ref, HEAD_DIM) + EPS) * qnw_ref[...]
    lane_q = lax.broadcasted_iota(jnp.int32, (1, D_ATTN), 1)
    first_q = (lane_q & (HEAD_DIM // 2 - 1)) < HEAD_DIM // 4
    q_out[...] = _rope(qn, cq_ref[...], sq_ref[...], first_q).astype(BF16)

    k = jnp.dot(h, w_ref[:, OFF_K:OFF_V], preferred_element_type=F32)
    kn = k * lax.rsqrt(_segment_mean_sq(k, segk_ref, HEAD_DIM) + EPS) * knw_ref[...]
    lane_k = lax.broadcasted_iota(jnp.int32, (1, D_KV), 1)
    first_k = (lane_k & (HEAD_DIM // 2 - 1)) < HEAD_DIM // 4
    kr = _rope(kn, ck_ref[...], sk_ref[...], first_k)
    ks = pltpu.roll(kr, HEAD_DIM, 1)
    low = lane_k < HEAD_DIM
    zero = jnp.zeros_like(kr)
    kp_out[0] = jnp.where(low, kr, zero).astype(BF16)
    kp_out[1] = jnp.where(low, zero, ks).astype(BF16)
    kp_out[2] = jnp.where(low, ks, zero).astype(BF16)
    kp_out[3] = jnp.where(low, zero, kr).astype(BF16)

    v = jnp.dot(h, w_ref[:, OFF_V:OFF_H], preferred_element_type=F32)
    va_out[...] = v.T.astype(BF16)

    hg_out[...] = jnp.dot(h, w_ref[:, OFF_H:OFF_C], preferred_element_type=F32)
    cv_out[...] = jnp.dot(h, w_ref[:, OFF_C:D_IN_PROJ], preferred_element_type=F32)


def _inproj(x2, nw, w_in, qnw, knw, tabs, segq, segk, batch, seq):
    tokens = x2.shape[0]
    n_seq_tiles = seq // TM_PROJ
    cq, sq, ck, sk = tabs
    tab_q = pl.BlockSpec((TM_PROJ, D_ATTN), lambda i: (i % n_seq_tiles, 0))
    tab_k = pl.BlockSpec((TM_PROJ, D_KV), lambda i: (i % n_seq_tiles, 0))
    row = lambda w: pl.BlockSpec((TM_PROJ, w), lambda i: (i, 0))
    return pl.pallas_call(
        _inproj_kernel,
        grid=(tokens // TM_PROJ,),
        in_specs=[row(D_MODEL), _full((1, D_MODEL)), _full((D_MODEL, D_IN_PROJ)),
                  _full((1, D_ATTN)), _full((1, D_KV)), tab_q, tab_q, tab_k, tab_k,
                  _full((D_ATTN, D_ATTN)), _full((D_KV, D_KV))],
        out_specs=[row(D_ATTN),
                   pl.BlockSpec((2 * N_KV_HEADS, TM_PROJ, D_KV), lambda i: (0, i, 0)),
                   pl.BlockSpec((None, D_KV, TM_PROJ), lambda i: (i // n_seq_tiles, 0, i % n_seq_tiles)),
                   row(5 * D_HGRN), row(2 * D_CONV)],
        out_shape=[jax.ShapeDtypeStruct((tokens, D_ATTN), BF16),
                   jax.ShapeDtypeStruct((2 * N_KV_HEADS, tokens, D_KV), BF16),
                   jax.ShapeDtypeStruct((batch, D_KV, seq), BF16),
                   jax.ShapeDtypeStruct((tokens, 5 * D_HGRN), F32),
                   jax.ShapeDtypeStruct((tokens, 2 * D_CONV), F32)],
        compiler_params=_params(1),
        name="inproj",
    )(x2, nw, w_in, qnw, knw, cq, sq, ck, sk, segq, segk)


def _attn_kernel(q_ref, kp_ref, va_ref, nw_ref, o_ref, st_ref, yt_ref):
    n_items = ATTN_SUBTILES * N_HEADS

    def scores_t(it):
        sub, hd = divmod(it, N_HEADS)
        g = hd // (N_HEADS // N_KV_HEADS)
        qp = q_ref[sub * TQ_ATTN:(sub + 1) * TQ_ATTN, (hd // 2) * LANES:(hd // 2 + 1) * LANES]
        st_ref[it % ATTN_SLOTS] = lax.dot_general(kp_ref[2 * g + hd % 2], qp, (((1,), (1,)), ((), ())),
                                                  preferred_element_type=F32)

    for it in range(ATTN_AHEAD):
        scores_t(it)
    for it in range(n_items):
        sub, hd = divmod(it, N_HEADS)
        g = hd // (N_HEADS // N_KV_HEADS)
        if it + ATTN_AHEAD < n_items:
            scores_t(it + ATTN_AHEAD)
        slot = it % ATTN_SLOTS
        m = jnp.max(st_ref[slot], axis=0, keepdims=True)
        p = jnp.exp2(st_ref[slot] - m)
        den = jnp.sum(p, axis=0, keepdims=True)
        num = jnp.dot(va_ref[g * HEAD_DIM:(g + 1) * HEAD_DIM, :], p.astype(BF16),
                      preferred_element_type=F32)
        yt_ref[hd * HEAD_DIM:(hd + 1) * HEAD_DIM, :] = num / den
        if hd == N_HEADS - 1:
            y = yt_ref[...].T
            ms = jnp.mean(y * y, axis=-1, keepdims=True)
            o_ref[sub * TQ_ATTN:(sub + 1) * TQ_ATTN, :] = (y * lax.rsqrt(ms + EPS) * nw_ref[...]).astype(BF16)


def _attention(q, kp, va, nw, batch, seq):
    tokens = q.shape[0]
    rows = ATTN_SUBTILES * TQ_ATTN
    nq = seq // rows
    return pl.pallas_call(
        _attn_kernel,
        grid=(batch, nq),
        in_specs=[pl.BlockSpec((rows, D_ATTN), lambda b, j: (b * nq + j, 0)),
                  pl.BlockSpec((2 * N_KV_HEADS, seq, D_KV), lambda b, j: (0, b, 0)),
                  pl.BlockSpec((None, D_KV, seq), lambda b, j: (b, 0, 0)),
                  pl.BlockSpec((1, D_ATTN), lambda b, j: (0, 0))],
        out_specs=pl.BlockSpec((rows, D_ATTN), lambda b, j: (b * nq + j, 0)),
        out_shape=jax.ShapeDtypeStruct((tokens, D_ATTN), BF16),
        scratch_shapes=[pltpu.VMEM((ATTN_SLOTS, seq, TQ_ATTN), F32), pltpu.VMEM((D_ATTN, TQ_ATTN), F32)],
        compiler_params=_params(2),
        name="attn",
    )(q, kp, va, nw)


HG_LEVELS = (32, 16, 8)
N_EXP_GROUPS = 2 + len(HG_LEVELS)


def _hgrn_constants():
    n = HG_CHUNK
    mats = []
    idx = np.arange(n)
    cum = (idx[None, :] <= idx[:, None]).astype(np.float32)
    mats.append(cum)
    for half in HG_LEVELS:
        m = np.zeros((n, n), np.float32)
        for p in range(n):
            if (p // half) % 2 == 1:
                p0 = (p // half) * half
                m[p, p0 + 1:p + 1] = 1.0
            else:
                p0 = (p // half) * half + half
                m[p, p + 1:p0 + 1] = 1.0
        mats.append(m)
    m = np.zeros((n, n), np.float32)
    for p in range(n):
        pr = (p // HG_FINE) * HG_FINE + HG_REF
        if p > pr:
            m[p, pr + 1:p + 1] = 1.0
        elif p < pr:
            m[p, p + 1:pr + 1] = -1.0
    mats.append(m)
    fwd = np.concatenate(mats, axis=0)
    bwd = np.concatenate([m[::-1, ::-1] for m in mats], axis=0)
    m2 = np.stack([np.tile(fwd, (1, 2)), np.tile(bwd, (1, 2))])

    lev = np.full((n, n), len(HG_LEVELS) + 1, np.int32)
    for t in range(n):
        for s in range(t + 1):
            if t // HG_FINE == s // HG_FINE:
                lev[t, s] = len(HG_LEVELS)
            else:
                for li, half in enumerate(HG_LEVELS):
                    if t // (2 * half) == s // (2 * half) and t // half != s // half:
                        lev[t, s] = li
                        break
    lev2 = np.stack([np.tile(lev, (1, HGRN_HEADS)), np.tile(lev[::-1, ::-1], (1, HGRN_HEADS))])
    head = np.arange(D_HGRN) // HEAD_DIM
    headmask = (head[:, None] == head[None, :]).astype(np.float32)
    return m2, lev2, headmask


def _hgrn_kernel(hq_ref, zf_ref, zb_ref, hi_ref, gate_ref, logit_ref, gnw_ref, m2_ref, lev_ref,
                 hmb_ref, seg_ref, o_ref, ofw_ref, obw_ref, st_ref, *, layer):
    seq = hq_ref.shape[0]
    n_chunks = seq // HG_CHUNK
    n = HG_CHUNK

    lg = [logit_ref[i] for i in range(DEPTH)]
    mx = functools.reduce(jnp.maximum, lg)
    ex = [jnp.exp(v - mx) for v in lg]
    den = functools.reduce(lambda a, b: a + b, ex)
    sm = [e / den for e in ex]
    acc = sm[0]
    for i in range(1, layer + 1):
        acc = acc + sm[i]
    lb = acc - sm[0]

    def stage_a(d, z_ref, lb_row, r0):
        z = z_ref[pl.ds(r0, n), :]
        f = lb_row + (1.0 - lb_row) * jax.nn.sigmoid(z)
        g = jnp.log(jnp.maximum(f, F_MIN)) * LOG2_E
        kk = 1.0 - f
        g1 = g.astype(BF16)
        g2 = (g - g1.astype(F32)).astype(BF16)
        gcat = jnp.concatenate([g1, g2], axis=0)
        expo = jnp.dot(m2_ref[d], gcat, preferred_element_type=F32)
        return kk, expo

    def stage_b(d, r0, kk, expo):
        hm_bf = hmb_ref[...]
        q = hq_ref[pl.ds(r0, n), :]
        v = hi_ref[pl.ds(r0, n), :]
        lev = lev_ref[d]
        a = jnp.zeros((n, D_HGRN), F32)
        for li in range(len(HG_LEVELS) + 1):
            e_lev = expo[(1 + li) * n:(2 + li) * n]
            x_q = jnp.exp2(e_lev)
            x_k = x_q if li < len(HG_LEVELS) else jnp.exp2(-e_lev)
            qt = (q * x_q).astype(BF16)
            kt = (kk * x_k).astype(BF16)
            kbd = jnp.concatenate([kt] * HGRN_HEADS, axis=0) * hm_bf
            sc = lax.dot_general(qt, kbd, (((1,), (1,)), ((), ())), preferred_element_type=F32)
            a = jnp.where(lev == li, sc, a)
        return a.astype(BF16)

    def stage_b2(d, r0, kk, expo, a):
        hm_bf = hmb_ref[...]
        q = hq_ref[pl.ds(r0, n), :]
        v = hi_ref[pl.ds(r0, n), :]
        vbd = jnp.concatenate([v.astype(BF16)] * HGRN_HEADS, axis=0) * hm_bf
        o_intra = jnp.dot(a, vbd, preferred_element_type=F32)

        e_cum = expo[0:n]
        e_tot = e_cum[n - 1:n] if d == 0 else e_cum[0:1]
        qb = (q * jnp.exp2(e_cum)).astype(BF16)
        kdec = (kk * jnp.exp2(e_tot - e_cum)).astype(BF16)
        zpad = jnp.zeros((LANES - n, D_HGRN), F32)
        vt = jnp.concatenate([v, zpad], axis=0).T.astype(BF16)
        kpad = jnp.concatenate([kdec, jnp.zeros((LANES - n, D_HGRN), BF16)], axis=0)
        upd = jnp.dot(vt, kpad, preferred_element_type=F32)
        return o_intra, qb, upd, jnp.exp2(e_tot)

    def stage_c(d, o_intra, qb, upd, decay_tot):
        st = st_ref[d]
        o = o_intra + lax.dot_general(qb, st.astype(BF16) * hmb_ref[...], (((1,), (1,)), ((), ())),
                                      preferred_element_type=F32)
        st_ref[d] = st * decay_tot + upd
        return o

    st_ref[...] = jnp.zeros_like(st_ref)

    def scan_body(ci, carry):
        items = []
        for u in range(HG_UNROLL):
            cf = ci * HG_UNROLL + u
            items.append((0, zf_ref, lb[0:1], pl.multiple_of(cf * n, n), ofw_ref))
            items.append((1, zb_ref, lb[1:2], pl.multiple_of((n_chunks - 1 - cf) * n, n), obw_ref))
        sa = [stage_a(d, z_ref, lb_row, r0) for d, z_ref, lb_row, r0, _ in items]
        sb = [stage_b(d, r0, *a) for (d, _, _, r0, _), a in zip(items, sa)]
        sb2 = [stage_b2(d, r0, *a, b) for (d, _, _, r0, _), a, b in zip(items, sa, sb)]
        for (d, _, _, r0, out_ref), b in zip(items, sb2):
            out_ref[pl.ds(r0, n), :] = stage_c(d, *b)
        return carry

    lax.fori_loop(0, n_chunks // HG_UNROLL, scan_body, 0)

    gnw = gnw_ref[...]

    def finish_body(i, carry):
        r0 = pl.multiple_of(i * HG_FINISH_ROWS, HG_FINISH_ROWS)
        o = ofw_ref[pl.ds(r0, HG_FINISH_ROWS), :] + obw_ref[pl.ds(r0, HG_FINISH_ROWS), :]
        on = o * lax.rsqrt(_segment_mean_sq(o, seg_ref, HEAD_DIM) + EPS) * gnw
        gate = gate_ref[pl.ds(r0, HG_FINISH_ROWS), :]
        o_ref[pl.ds(r0, HG_FINISH_ROWS), :] = (on * (gate * jax.nn.sigmoid(gate))).astype(BF16)
        return carry

    lax.fori_loop(0, seq // HG_FINISH_ROWS, finish_body, 0)


def _hgrn(hg, logits, gnw, consts, seg, layer, batch, seq):
    tokens = hg.shape[0]
    m2, lev, hmb = consts
    col = lambda j: pl.BlockSpec((seq, D_HGRN), lambda b: (b, j))
    return pl.pallas_call(
        functools.partial(_hgrn_kernel, layer=layer),
        grid=(batch,),
        in_specs=[col(0), col(1), col(2), col(3), col(4),
                  _full(logits.shape), _full((1, D_HGRN)), _full(m2.shape), _full(lev.shape),
                  _full(hmb.shape), _full(seg.shape)],
        out_specs=pl.BlockSpec((seq, D_HGRN), lambda b: (b, 0)),
        out_shape=jax.ShapeDtypeStruct((tokens, D_HGRN), BF16),
        scratch_shapes=[pltpu.VMEM((seq, D_HGRN), F32), pltpu.VMEM((seq, D_HGRN), F32),
                        pltpu.VMEM((2, D_HGRN, D_HGRN), F32)],
        compiler_params=_params(1),
        name="hgrn",
    )(hg, hg, hg, hg, hg, logits, gnw, m2, lev, hmb, seg)


def _conv_kernel(a_ref, b_ref, dww_ref, dwb_ref, lnw_ref, lnb_ref, pww_ref, pwb_ref, nw_ref,
                 o_ref, pad_ref, dw_ref):
    seq = a_ref.shape[0]
    halo = jnp.zeros((CONV_HALO, D_CONV), F32)
    pad_ref[0:CONV_HALO, :] = halo
    pad_ref[CONV_HALO + seq:CONV_HALO + seq + CONV_HALO, :] = halo
    pad_ref[CONV_HALO:CONV_HALO + seq, :] = a_ref[...] * jax.nn.sigmoid(b_ref[...])

    win_rows = CONV_ROWS + 2 * CONV_HALO

    def tap_body(i, carry):
        r0 = pl.multiple_of(i * CONV_ROWS, CONV_ROWS)
        win = pad_ref[pl.ds(r0, win_rows), :]
        acc = jnp.zeros((CONV_ROWS, D_CONV), F32)
        for r in range(SUBLANES):
            shifted = win if r == 0 else pltpu.roll(win, win_rows - r, 0)
            for a8 in range(0, 2 * CONV_HALO, SUBLANES):
                tap = a8 + r - (CONV_HALO - CONV_PAD)
                if 0 <= tap < CONV_WIDTH:
                    acc = acc + shifted[a8:a8 + CONV_ROWS] * dww_ref[tap:tap + 1, :]
        dw_ref[pl.ds(r0, CONV_ROWS), :] = acc + dwb_ref[...]
        return carry

    lax.fori_loop(0, seq // CONV_ROWS, tap_body, 0, unroll=2)

    def post_body(i, carry):
        r0 = pl.multiple_of(i * CONV_POST_ROWS, CONV_POST_ROWS)
        u = dw_ref[pl.ds(r0, CONV_POST_ROWS), :]
        mu = jnp.mean(u, axis=-1, keepdims=True)
        uc = u - mu
        var = jnp.mean(uc * uc, axis=-1, keepdims=True)
        y = uc * lax.rsqrt(var + LN_EPS) * lnw_ref[...] + lnb_ref[...]
        y = y * jax.nn.sigmoid(y)
        y = jnp.dot(y.astype(BF16), pww_ref[...], preferred_element_type=F32) + pwb_ref[...]
        ms = jnp.mean(y * y, axis=-1, keepdims=True)
        o_ref[pl.ds(r0, CONV_POST_ROWS), :] = (y * lax.rsqrt(ms + EPS) * nw_ref[...]).astype(BF16)
        return carry

    lax.fori_loop(0, seq // CONV_POST_ROWS, post_body, 0, unroll=2)


def _conv(cv, dww, dwb, lnw, lnb, pww, pwb, nw, batch, seq):
    tokens = cv.shape[0]
    col = lambda j: pl.BlockSpec((seq, D_CONV), lambda b: (b, j))
    vec = _full((1, D_CONV))
    return pl.pallas_call(
        _conv_kernel,
        grid=(batch,),
        in_specs=[col(0), col(1), _full(dww.shape), vec, vec, vec, _full((D_CONV, D_CONV)), vec, vec],
        out_specs=pl.BlockSpec((seq, D_CONV), lambda b: (b, 0)),
        out_shape=jax.ShapeDtypeStruct((tokens, D_CONV), BF16),
        scratch_shapes=[pltpu.VMEM((seq + 2 * CONV_HALO, D_CONV), F32), pltpu.VMEM((seq, D_CONV), F32)],
        compiler_params=_params(1),
        name="conv",
    )(cv, cv, dww, dwb, lnw, lnb, pww, pwb, nw)


def _ffn_kernel(x_ref, ya_ref, yh_ref, yc_ref, wo_ref, nw_ref, wg_ref, wu_ref, wd_ref,
                o_ref, act_ref):
    x1 = x_ref[...]
    x1 = x1 + jnp.dot(ya_ref[...], wo_ref[0:D_ATTN, :], preferred_element_type=F32)
    x1 = x1 + jnp.dot(yh_ref[...], wo_ref[D_ATTN:D_ATTN + D_HGRN, :], preferred_element_type=F32)
    x1 = x1 + jnp.dot(yc_ref[...], wo_ref[D_ATTN + D_HGRN:D_MODEL, :], preferred_element_type=F32)
    ms = jnp.mean(x1 * x1, axis=-1, keepdims=True)
    h = (x1 * lax.rsqrt(ms + EPS) * nw_ref[...]).astype(BF16)
    for c in range(0, D_FF, FF_CHUNK):
        g = jnp.dot(h, wg_ref[:, c:c + FF_CHUNK], preferred_element_type=F32)
        u = jnp.dot(h, wu_ref[:, c:c + FF_CHUNK], preferred_element_type=F32)
        act_ref[:, c:c + FF_CHUNK] = (g * jax.nn.sigmoid(g) * u).astype(BF16)
    o_ref[...] = x1 + jnp.dot(act_ref[...], wd_ref[...], preferred_element_type=F32)


def _ffn(x2, ya, yh, yc, wo, nw, wg, wu, wd):
    tokens = x2.shape[0]
    row = lambda w: pl.BlockSpec((TM_FFN, w), lambda i: (i, 0))
    resident = lambda shape: pl.BlockSpec(shape, lambda i: (0, 0), pipeline_mode=pl.Buffered(1))
    return pl.pallas_call(
        _ffn_kernel,
        grid=(tokens // TM_FFN,),
        in_specs=[row(D_MODEL), row(D_ATTN), row(D_HGRN), row(D_CONV),
                  resident((D_MODEL, D_MODEL)), _full((1, D_MODEL)),
                  resident((D_MODEL, D_FF)), resident((D_MODEL, D_FF)), resident((D_FF, D_MODEL))],
        out_specs=row(D_MODEL),
        out_shape=jax.ShapeDtypeStruct((tokens, D_MODEL), F32),
        scratch_shapes=[pltpu.VMEM((TM_FFN, D_FF), BF16)],
        compiler_params=_params(1),
        name="ffn",
    )(x2, ya, yh, yc, wo, nw, wg, wu, wd)


def _rope_tables(seq):
    rows = seq // GRID_W
    row_id = jnp.repeat(jnp.arange(rows, dtype=F32), GRID_W)
    col_id = jnp.tile(jnp.arange(GRID_W, dtype=F32), rows)
    half = HEAD_DIM // 2
    inv_freq = ROPE_THETA ** (-jnp.arange(0, half, 2, dtype=F32) / half)
    ang_r = row_id[:, None] * inv_freq[None, :]
    ang_c = col_id[:, None] * inv_freq[None, :]
    ang = jnp.concatenate([ang_r, ang_r, ang_c, ang_c], axis=-1)
    cos, sin = jnp.cos(ang), jnp.sin(ang)
    lane = np.arange(HEAD_DIM)
    sign = jnp.asarray(np.where((lane % (HEAD_DIM // 2)) < HEAD_DIM // 4, -1.0, 1.0), F32)
    sin_signed = sin * sign[None, :]
    scale = HEAD_DIM ** -0.5 * np.log2(np.e)
    return (jnp.tile(cos, (1, N_HEADS)) * scale, jnp.tile(sin_signed, (1, N_HEADS)) * scale,
            jnp.tile(cos, (1, N_KV_HEADS)), jnp.tile(sin_signed, (1, N_KV_HEADS)))


def _segment_matrix(width, seg_len):
    seg = np.arange(width) // seg_len
    return jnp.asarray((seg[:, None] == seg[None, :]).astype(np.float32), BF16)


def kernel(x, mix_norm_w, w_in, q_norm_w, k_norm_w, hgrn_lb_logits, hgrn_gnorm_w, conv_dw_w, conv_dw_b,
           conv_ln_w, conv_ln_b, conv_pw_w, conv_pw_b, attn_out_norm_w, conv_out_norm_w, w_out,
           ffn_norm_w, w_gate, w_up, w_down):
    batch, seq, d_model = x.shape
    assert d_model == D_MODEL and seq % TM_PROJ == 0 and seq % HG_CHUNK == 0
    assert seq % (ATTN_SUBTILES * TQ_ATTN) == 0 and seq % CONV_POST_ROWS == 0 and seq % HG_FINISH_ROWS == 0
    assert (batch * seq) % TM_FFN == 0 and w_in.shape == (DEPTH, D_MODEL, D_IN_PROJ)
    assert w_gate.shape == (DEPTH, D_MODEL, D_FF)

    tabs = _rope_tables(seq)
    segq = _segment_matrix(D_ATTN, HEAD_DIM)
    segk = _segment_matrix(D_KV, HEAD_DIM)
    segh = _segment_matrix(D_HGRN, HEAD_DIM)
    m2, lev, hm = _hgrn_constants()
    hconsts = (jnp.asarray(m2, BF16), jnp.asarray(lev), jnp.asarray(hm, BF16))
    logits = hgrn_lb_logits.astype(F32)

    x2 = x.reshape(batch * seq, D_MODEL)
    for l in range(DEPTH):
        q, kp, va, hg, cv = _inproj(
            x2, mix_norm_w[l][None, :], w_in[l].astype(BF16),
            jnp.tile(q_norm_w[l], N_HEADS)[None, :], jnp.tile(k_norm_w[l], N_KV_HEADS)[None, :],
            tabs, segq, segk, batch, seq)
        ya = _attention(q, kp, va, attn_out_norm_w[l][None, :], batch, seq)
        yh = _hgrn(hg, logits, jnp.tile(hgrn_gnorm_w[l], HGRN_HEADS)[None, :], hconsts, segh, l, batch, seq)
        yc = _conv(cv, conv_dw_w[l], conv_dw_b[l][None, :], conv_ln_w[l][None, :], conv_ln_b[l][None, :],
                   conv_pw_w[l].astype(BF16), conv_pw_b[l][None, :], conv_out_norm_w[l][None, :], batch, seq)
        x2 = _ffn(x2, ya, yh, yc, w_out[l].astype(BF16), ffn_norm_w[l][None, :],
                  w_gate[l].astype(BF16), w_up[l].astype(BF16), w_down[l].astype(BF16))
    return x2.reshape(batch, seq, D_MODEL)
```

```python
import functools

import numpy as np
import jax
import jax.numpy as jnp
from jax import lax
from jax.experimental import pallas as pl
from jax.experimental.pallas import tpu as pltpu

F32 = jnp.float32
BF16 = jnp.bfloat16

D_MODEL = 1024
DEPTH = 2
GRID_W = 64
D_ATTN = 512
D_HGRN = 256
D_CONV = 256
HEAD_DIM = 64
N_HEADS = D_ATTN // HEAD_DIM
N_KV_HEADS = 2
D_KV = N_KV_HEADS * HEAD_DIM
ROPE_THETA = 10000.0
HGRN_HEADS = D_HGRN // HEAD_DIM
F_MIN = 1e-6
LOG2_E = float(np.log2(np.e))
CONV_WIDTH = 31
CONV_PAD = (CONV_WIDTH - 1) // 2
D_FF = 2816
EPS = 1e-6
LN_EPS = 1e-5
D_IN_PROJ = D_ATTN + 2 * D_KV + 5 * D_HGRN + 2 * D_CONV

OFF_Q = 0
OFF_K = OFF_Q + D_ATTN
OFF_V = OFF_K + D_KV
OFF_H = OFF_V + D_KV
OFF_C = OFF_H + 5 * D_HGRN

LANES = 128
SUBLANES = 8
VMEM_LIMIT_BYTES = 56 * 1024 * 1024

TM_PROJ = 512
TQ_ATTN = 256
ATTN_SUBTILES = 4
ATTN_AHEAD = 3
ATTN_SLOTS = ATTN_AHEAD + 1
TM_FFN = 512
FF_CHUNK = 256
HG_CHUNK = 64
HG_FINE = 8
HG_REF = 2
HG_FINISH_ROWS = 256
HG_UNROLL = 4
CONV_ROWS = 64
CONV_POST_ROWS = 256
CONV_HALO = 16

assert -np.log(F_MIN) * (HG_FINE - 1 - HG_REF) < 80.0 and -np.log(F_MIN) * HG_REF < 80.0


def _params(n_grid_axes):
    return pltpu.CompilerParams(
        dimension_semantics=("arbitrary",) * n_grid_axes,
        vmem_limit_bytes=VMEM_LIMIT_BYTES)


def _full(shape):
    nd = len(shape)
    return pl.BlockSpec(shape, lambda *_: (0,) * nd)


def _segment_mean_sq(y, seg_ref, seg_len):
    return jnp.dot((y * y).astype(BF16), seg_ref[...], preferred_element_type=F32) * (1.0 / seg_len)


def _rope(xn, cos, sin_signed, first_mask):
    width = xn.shape[1]
    left = pltpu.roll(xn, width - HEAD_DIM // 4, 1)
    right = pltpu.roll(xn, HEAD_DIM // 4, 1)
    rot = jnp.where(first_mask, left, right)
    return xn * cos + rot * sin_signed


def _inproj_kernel(x_ref, nw_ref, w_ref, qnw_ref, knw_ref, cq_ref, sq_ref, ck_ref, sk_ref,
                   segq_ref, segk_ref,
                   q_out, kp_out, va_out, hg_out, cv_out):
    x = x_ref[...]
    ms = jnp.mean(x * x, axis=-1, keepdims=True)
    h = (x * lax.rsqrt(ms + EPS) * nw_ref[...]).astype(BF16)

    q = jnp.dot(h, w_ref[:, OFF_Q:OFF_K], preferred_element_type=F32)
    qn = q * lax.rsqrt(_segment_mean_sq(q, segq_ref, HEAD_DIM) + EPS) * qnw_ref[...]
    lane_q = lax.broadcasted_iota(jnp.int32, (1, D_ATTN), 1)
    first_q = (lane_q & (HEAD_DIM // 2 - 1)) < HEAD_DIM // 4
    q_out[...] = _rope(qn, cq_ref[...], sq_ref[...], first_q).T.astype(BF16)

    k = jnp.dot(h, w_ref[:, OFF_K:OFF_V], preferred_element_type=F32)
    kn = k * lax.rsqrt(_segment_mean_sq(k, segk_ref, HEAD_DIM) + EPS) * knw_ref[...]
    lane_k = lax.broadcasted_iota(jnp.int32, (1, D_KV), 1)
    first_k = (lane_k & (HEAD_DIM // 2 - 1)) < HEAD_DIM // 4
    kr = _rope(kn, ck_ref[...], sk_ref[...], first_k)
    ks = pltpu.roll(kr, HEAD_DIM, 1)
    low = lane_k < HEAD_DIM
    zero = jnp.zeros_like(kr)
    kp_out[0] = jnp.where(low, kr, zero).astype(BF16)
    kp_out[1] = jnp.where(low, zero, ks).astype(BF16)
    kp_out[2] = jnp.where(low, ks, zero).astype(BF16)
    kp_out[3] = jnp.where(low, zero, kr).astype(BF16)

    v = jnp.dot(h, w_ref[:, OFF_V:OFF_H], preferred_element_type=F32)
    va_out[...] = v.T.astype(BF16)

    hg_out[...] = jnp.dot(h, w_ref[:, OFF_H:OFF_C], preferred_element_type=F32)
    cv_out[...] = jnp.dot(h, w_ref[:, OFF_C:D_IN_PROJ], preferred_element_type=F32)


def _inproj(x2, nw, w_in, qnw, knw, tabs, segq, segk, batch, seq):
    tokens = x2.shape[0]
    n_seq_tiles = seq // TM_PROJ
    cq, sq, ck, sk = tabs
    tab_q = pl.BlockSpec((TM_PROJ, D_ATTN), lambda i: (i % n_seq_tiles, 0))
    tab_k = pl.BlockSpec((TM_PROJ, D_KV), lambda i: (i % n_seq_tiles, 0))
    row = lambda w: pl.BlockSpec((TM_PROJ, w), lambda i: (i, 0))
    return pl.pallas_call(
        _inproj_kernel,
        grid=(tokens // TM_PROJ,),
        in_specs=[row(D_MODEL), _full((1, D_MODEL)), _full((D_MODEL, D_IN_PROJ)),
                  _full((1, D_ATTN)), _full((1, D_KV)), tab_q, tab_q, tab_k, tab_k,
                  _full((D_ATTN, D_ATTN)), _full((D_KV, D_KV))],
        out_specs=[pl.BlockSpec((None, D_ATTN, TM_PROJ), lambda i: (i // n_seq_tiles, 0, i % n_seq_tiles)),
                   pl.BlockSpec((2 * N_KV_HEADS, TM_PROJ, D_KV), lambda i: (0, i, 0)),
                   pl.BlockSpec((None, D_KV, TM_PROJ), lambda i: (i // n_seq_tiles, 0, i % n_seq_tiles)),
                   row(5 * D_HGRN), row(2 * D_CONV)],
        out_shape=[jax.ShapeDtypeStruct((batch, D_ATTN, seq), BF16),
                   jax.ShapeDtypeStruct((2 * N_KV_HEADS, tokens, D_KV), BF16),
                   jax.ShapeDtypeStruct((batch, D_KV, seq), BF16),
                   jax.ShapeDtypeStruct((tokens, 5 * D_HGRN), F32),
                   jax.ShapeDtypeStruct((tokens, 2 * D_CONV), F32)],
        compiler_params=_params(1),
        name="inproj",
    )(x2, nw, w_in, qnw, knw, cq, sq, ck, sk, segq, segk)


def _attn_kernel(q_ref, kp_ref, va_ref, nw_ref, o_ref, st_ref, yt_ref):
    n_items = ATTN_SUBTILES * N_HEADS

    def scores_t(it):
        sub, hd = divmod(it, N_HEADS)
        g = hd // (N_HEADS // N_KV_HEADS)
        qp = q_ref[(hd // 2) * LANES:(hd // 2 + 1) * LANES, sub * TQ_ATTN:(sub + 1) * TQ_ATTN]
        st_ref[it % ATTN_SLOTS] = jnp.dot(kp_ref[2 * g + hd % 2], qp, preferred_element_type=F32)

    for it in range(ATTN_AHEAD):
        scores_t(it)
    for it in range(n_items):
        sub, hd = divmod(it, N_HEADS)
        g = hd // (N_HEADS // N_KV_HEADS)
        if it + ATTN_AHEAD < n_items:
            scores_t(it + ATTN_AHEAD)
        slot = it % ATTN_SLOTS
        m = jnp.max(st_ref[slot], axis=0, keepdims=True)
        p = jnp.exp2(st_ref[slot] - m)
        den = jnp.sum(p, axis=0, keepdims=True)
        num = jnp.dot(va_ref[g * HEAD_DIM:(g + 1) * HEAD_DIM, :], p.astype(BF16),
                      preferred_element_type=F32)
        yt_ref[hd * HEAD_DIM:(hd + 1) * HEAD_DIM, :] = num / den
        if hd == N_HEADS - 1:
            y = yt_ref[...].T
            ms = jnp.mean(y * y, axis=-1, keepdims=True)
            o_ref[sub * TQ_ATTN:(sub + 1) * TQ_ATTN, :] = (y * lax.rsqrt(ms + EPS) * nw_ref[...]).astype(BF16)


def _attention(q, kp, va, nw, batch, seq):
    tokens = batch * seq
    rows = ATTN_SUBTILES * TQ_ATTN
    nq = seq // rows
    return pl.pallas_call(
        _attn_kernel,
        grid=(batch, nq),
        in_specs=[pl.BlockSpec((None, D_ATTN, rows), lambda b, j: (b, 0, j)),
                  pl.BlockSpec((2 * N_KV_HEADS, seq, D_KV), lambda b, j: (0, b, 0)),
                  pl.BlockSpec((None, D_KV, seq), lambda b, j: (b, 0, 0)),
                  pl.BlockSpec((1, D_ATTN), lambda b, j: (0, 0))],
        out_specs=pl.BlockSpec((rows, D_ATTN), lambda b, j: (b * nq + j, 0)),
        out_shape=jax.ShapeDtypeStruct((tokens, D_ATTN), BF16),
        scratch_shapes=[pltpu.VMEM((ATTN_SLOTS, seq, TQ_ATTN), F32), pltpu.VMEM((D_ATTN, TQ_ATTN), F32)],
        compiler_params=_params(2),
        name="attn",
    )(q, kp, va, nw)


HG_LEVELS = (32, 16, 8)


def _hgrn_constants():
    n = HG_CHUNK
    idx = np.arange(n)
    fwd = (idx[None, :] <= idx[:, None]).astype(np.float32)
    bwd = fwd[::-1, ::-1]
    m2 = np.stack([np.tile(fwd, (1, 2)), np.tile(bwd, (1, 2))])

    lev = np.full((n, n), len(HG_LEVELS) + 1, np.int32)
    for t in range(n):
        for s in range(t + 1):
            if t // HG_FINE == s // HG_FINE:
                lev[t, s] = len(HG_LEVELS)
            else:
                for li, half in enumerate(HG_LEVELS):
                    if t // (2 * half) == s // (2 * half) and t // half != s // half:
                        lev[t, s] = li
                        break
    lev2 = np.stack([np.tile(lev, (1, HGRN_HEADS)), np.tile(lev[::-1, ::-1], (1, HGRN_HEADS))])
    head = np.arange(D_HGRN) // HEAD_DIM
    headmask = (head[:, None] == head[None, :]).astype(np.float32)
    return m2, lev2, headmask


def _hgrn_kernel(hq_ref, zf_ref, zb_ref, hi_ref, gate_ref, logit_ref, gnw_ref, m2_ref, lev_ref,
                 hmb_ref, seg_ref, o_ref, ofw_ref, obw_ref, st_ref, *, layer):
    seq = hq_ref.shape[0]
    n_chunks = seq // HG_CHUNK
    n = HG_CHUNK

    lg = [logit_ref[i] for i in range(DEPTH)]
    mx = functools.reduce(jnp.maximum, lg)
    ex = [jnp.exp(v - mx) for v in lg]
    den = functools.reduce(lambda a, b: a + b, ex)
    sm = [e / den for e in ex]
    acc = sm[0]
    for i in range(1, layer + 1):
        acc = acc + sm[i]
    lb = acc - sm[0]

    def stage_a(d, z_ref, lb_row, r0):
        z = z_ref[pl.ds(r0, n), :]
        f = lb_row + (1.0 - lb_row) * jax.nn.sigmoid(z)
        g = jnp.log(jnp.maximum(f, F_MIN)) * LOG2_E
        kk = 1.0 - f
        g1 = g.astype(BF16)
        g2 = (g - g1.astype(F32)).astype(BF16)
        gcat = jnp.concatenate([g1, g2], axis=0)
        e_cum = jnp.dot(m2_ref[d], gcat, preferred_element_type=F32)
        return kk, e_cum

    def level_exponent(e_cum, d, li):
        fine = li == len(HG_LEVELS)
        blk = HG_FINE if fine else HG_LEVELS[li]
        parts = []
        for r in range(0, n, blk):
            p_start = r if d == 0 else n - blk - r
            if fine:
                p_ref, later = p_start + HG_REF, True
            else:
                p_ref = (p_start // (2 * blk)) * 2 * blk + blk
                later = p_start >= p_ref
            t_ref = p_ref if d == 0 else n - 1 - p_ref
            ref = e_cum[t_ref:t_ref + 1]
            rows = e_cum[r:r + blk]
            parts.append(rows - ref if later else ref - rows)
        return jnp.concatenate(parts, axis=0)

    def stage_b(d, r0, kk, e_cum):
        hm_bf = hmb_ref[...]
        q = hq_ref[pl.ds(r0, n), :]
        v = hi_ref[pl.ds(r0, n), :]
        lev = lev_ref[d]
        a = jnp.zeros((n, D_HGRN), F32)
        for li in range(len(HG_LEVELS) + 1):
            e_lev = level_exponent(e_cum, d, li)
            x_q = jnp.exp2(e_lev)
            x_k = x_q if li < len(HG_LEVELS) else jnp.exp2(-e_lev)
            qt = (q * x_q).astype(BF16)
            kt = (kk * x_k).astype(BF16)
            kbd = jnp.concatenate([kt] * HGRN_HEADS, axis=0) * hm_bf
            sc = lax.dot_general(qt, kbd, (((1,), (1,)), ((), ())), preferred_element_type=F32)
            a = jnp.where(lev == li, sc, a)
        return a.astype(BF16)

    def stage_b2(d, r0, kk, e_cum, a):
        hm_bf = hmb_ref[...]
        q = hq_ref[pl.ds(r0, n), :]
        v = hi_ref[pl.ds(r0, n), :]
        vbd = jnp.concatenate([v.astype(BF16)] * HGRN_HEADS, axis=0) * hm_bf
        o_intra = jnp.dot(a, vbd, preferred_element_type=F32)

        e_tot = e_cum[n - 1:n] if d == 0 else e_cum[0:1]
        qb = (q * jnp.exp2(e_cum)).astype(BF16)
        kdec = (kk * jnp.exp2(e_tot - e_cum)).astype(BF16)
        zpad = jnp.zeros((LANES - n, D_HGRN), F32)
        vt = jnp.concatenate([v, zpad], axis=0).T.astype(BF16)
        kpad = jnp.concatenate([kdec, jnp.zeros((LANES - n, D_HGRN), BF16)], axis=0)
        upd = jnp.dot(vt, kpad, preferred_element_type=F32)
        return o_intra, qb, upd, jnp.exp2(e_tot)

    def stage_c(d, o_intra, qb, upd, decay_tot):
        st = st_ref[d]
        o = o_intra + lax.dot_general(qb, st.astype(BF16) * hmb_ref[...], (((1,), (1,)), ((), ())),
                                      preferred_element_type=F32)
        st_ref[d] = st * decay_tot + upd
        return o

    st_ref[...] = jnp.zeros_like(st_ref)

    def scan_body(ci, carry):
        items = []
        for u in range(HG_UNROLL):
            cf = ci * HG_UNROLL + u
            items.append((0, zf_ref, lb[0:1], pl.multiple_of(cf * n, n), ofw_ref))
            items.append((1, zb_ref, lb[1:2], pl.multiple_of((n_chunks - 1 - cf) * n, n), obw_ref))
        sa = [stage_a(d, z_ref, lb_row, r0) for d, z_ref, lb_row, r0, _ in items]
        sb = [stage_b(d, r0, *a) for (d, _, _, r0, _), a in zip(items, sa)]
        sb2 = [stage_b2(d, r0, *a, b) for (d, _, _, r0, _), a, b in zip(items, sa, sb)]
        for (d, _, _, r0, out_ref), b in zip(items, sb2):
            out_ref[pl.ds(r0, n), :] = stage_c(d, *b)
        return carry

    lax.fori_loop(0, n_chunks // HG_UNROLL, scan_body, 0)

    gnw = gnw_ref[...]

    def finish_body(i, carry):
        r0 = pl.multiple_of(i * HG_FINISH_ROWS, HG_FINISH_ROWS)
        o = ofw_ref[pl.ds(r0, HG_FINISH_ROWS), :] + obw_ref[pl.ds(r0, HG_FINISH_ROWS), :]
        on = o * lax.rsqrt(_segment_mean_sq(o, seg_ref, HEAD_DIM) + EPS) * gnw
        gate = gate_ref[pl.ds(r0, HG_FINISH_ROWS), :]
        o_ref[pl.ds(r0, HG_FINISH_ROWS), :] = (on * (gate * jax.nn.sigmoid(gate))).astype(BF16)
        return carry

    lax.fori_loop(0, seq // HG_FINISH_ROWS, finish_body, 0)


def _hgrn(hg, logits, gnw, consts, seg, layer, batch, seq):
    tokens = hg.shape[0]
    m2, lev, hmb = consts
    col = lambda j: pl.BlockSpec((seq, D_HGRN), lambda b: (b, j))
    return pl.pallas_call(
        functools.partial(_hgrn_kernel, layer=layer),
        grid=(batch,),
        in_specs=[col(0), col(1), col(2), col(3), col(4),
                  _full(logits.shape), _full((1, D_HGRN)), _full(m2.shape), _full(lev.shape),
                  _full(hmb.shape), _full(seg.shape)],
        out_specs=pl.BlockSpec((seq, D_HGRN), lambda b: (b, 0)),
        out_shape=jax.ShapeDtypeStruct((tokens, D_HGRN), BF16),
        scratch_shapes=[pltpu.VMEM((seq, D_HGRN), F32), pltpu.VMEM((seq, D_HGRN), F32),
                        pltpu.VMEM((2, D_HGRN, D_HGRN), F32)],
        compiler_params=_params(1),
        name="hgrn",
    )(hg, hg, hg, hg, hg, logits, gnw, m2, lev, hmb, seg)


def _conv_kernel(a_ref, b_ref, dww_ref, dwb_ref, lnw_ref, lnb_ref, pww_ref, pwb_ref, nw_ref,
                 o_ref, pad_ref, dw_ref):
    seq = a_ref.shape[0]
    halo = jnp.zeros((CONV_HALO, D_CONV), F32)
    pad_ref[0:CONV_HALO, :] = halo
    pad_ref[CONV_HALO + seq:CONV_HALO + seq + CONV_HALO, :] = halo
    pad_ref[CONV_HALO:CONV_HALO + seq, :] = a_ref[...] * jax.nn.sigmoid(b_ref[...])

    win_rows = CONV_ROWS + 2 * CONV_HALO

    def tap_body(i, carry):
        r0 = pl.multiple_of(i * CONV_ROWS, CONV_ROWS)
        win = pad_ref[pl.ds(r0, win_rows), :]
        acc = jnp.zeros((CONV_ROWS, D_CONV), F32)
        for r in range(SUBLANES):
            shifted = win if r == 0 else pltpu.roll(win, win_rows - r, 0)
            for a8 in range(0, 2 * CONV_HALO, SUBLANES):
                tap = a8 + r - (CONV_HALO - CONV_PAD)
                if 0 <= tap < CONV_WIDTH:
                    acc = acc + shifted[a8:a8 + CONV_ROWS] * dww_ref[tap:tap + 1, :]
        dw_ref[pl.ds(r0, CONV_ROWS), :] = acc + dwb_ref[...]
        return carry

    lax.fori_loop(0, seq // CONV_ROWS, tap_body, 0, unroll=2)

    def post_body(i, carry):
        r0 = pl.multiple_of(i * CONV_POST_ROWS, CONV_POST_ROWS)
        u = dw_ref[pl.ds(r0, CONV_POST_ROWS), :]
        mu = jnp.mean(u, axis=-1, keepdims=True)
        uc = u - mu
        var = jnp.mean(uc * uc, axis=-1, keepdims=True)
        y = uc * lax.rsqrt(var + LN_EPS) * lnw_ref[...] + lnb_ref[...]
        y = y * jax.nn.sigmoid(y)
        y = jnp.dot(y.astype(BF16), pww_ref[...], preferred_element_type=F32) + pwb_ref[...]
        ms = jnp.mean(y * y, axis=-1, keepdims=True)
        o_ref[pl.ds(r0, CONV_POST_ROWS), :] = (y * lax.rsqrt(ms + EPS) * nw_ref[...]).astype(BF16)
        return carry

    lax.fori_loop(0, seq // CONV_POST_ROWS, post_body, 0, unroll=2)


def _conv(cv, dww, dwb, lnw, lnb, pww, pwb, nw, batch, seq):
    tokens = cv.shape[0]
    col = lambda j: pl.BlockSpec((seq, D_CONV), lambda b: (b, j))
    vec = _full((1, D_CONV))
    return pl.pallas_call(
        _conv_kernel,
        grid=(batch,),
        in_specs=[col(0), col(1), _full(dww.shape), vec, vec, vec, _full((D_CONV, D_CONV)), vec, vec],
        out_specs=pl.BlockSpec((seq, D_CONV), lambda b: (b, 0)),
        out_shape=jax.ShapeDtypeStruct((tokens, D_CONV), BF16),
        scratch_shapes=[pltpu.VMEM((seq + 2 * CONV_HALO, D_CONV), F32), pltpu.VMEM((seq, D_CONV), F32)],
        compiler_params=_params(1),
        name="conv",
    )(cv, cv, dww, dwb, lnw, lnb, pww, pwb, nw)


def _ffn_kernel(x_ref, ya_ref, yh_ref, yc_ref, wo_ref, nw_ref, wg_ref, wu_ref, wd_ref,
                o_ref, act_ref):
    x1 = x_ref[...]
    x1 = x1 + jnp.dot(ya_ref[...], wo_ref[0:D_ATTN, :], preferred_element_type=F32)
    x1 = x1 + jnp.dot(yh_ref[...], wo_ref[D_ATTN:D_ATTN + D_HGRN, :], preferred_element_type=F32)
    x1 = x1 + jnp.dot(yc_ref[...], wo_ref[D_ATTN + D_HGRN:D_MODEL, :], preferred_element_type=F32)
    ms = jnp.mean(x1 * x1, axis=-1, keepdims=True)
    h = (x1 * lax.rsqrt(ms + EPS) * nw_ref[...]).astype(BF16)
    for c in range(0, D_FF, FF_CHUNK):
        g = jnp.dot(h, wg_ref[:, c:c + FF_CHUNK], preferred_element_type=F32)
        u = jnp.dot(h, wu_ref[:, c:c + FF_CHUNK], preferred_element_type=F32)
        act_ref[:, c:c + FF_CHUNK] = (g * jax.nn.sigmoid(g) * u).astype(BF16)
    o_ref[...] = x1 + jnp.dot(act_ref[...], wd_ref[...], preferred_element_type=F32)


def _ffn(x2, ya, yh, yc, wo, nw, wg, wu, wd):
    tokens = x2.shape[0]
    row = lambda w: pl.BlockSpec((TM_FFN, w), lambda i: (i, 0))
    resident = lambda shape: pl.BlockSpec(shape, lambda i: (0, 0), pipeline_mode=pl.Buffered(1))
    return pl.pallas_call(
        _ffn_kernel,
        grid=(tokens // TM_FFN,),
        in_specs=[row(D_MODEL), row(D_ATTN), row(D_HGRN), row(D_CONV),
                  resident((D_MODEL, D_MODEL)), _full((1, D_MODEL)),
                  resident((D_MODEL, D_FF)), resident((D_MODEL, D_FF)), resident((D_FF, D_MODEL))],
        out_specs=row(D_MODEL),
        out_shape=jax.ShapeDtypeStruct((tokens, D_MODEL), F32),
        scratch_shapes=[pltpu.VMEM((TM_FFN, D_FF), BF16)],
        compiler_params=_params(1),
        name="ffn",
    )(x2, ya, yh, yc, wo, nw, wg, wu, wd)


def _rope_tables(seq):
    rows = seq // GRID_W
    row_id = jnp.repeat(jnp.arange(rows, dtype=F32), GRID_W)
    col_id = jnp.tile(jnp.arange(GRID_W, dtype=F32), rows)
    half = HEAD_DIM // 2
    inv_freq = ROPE_THETA ** (-jnp.arange(0, half, 2, dtype=F32) / half)
    ang_r = row_id[:, None] * inv_freq[None, :]
    ang_c = col_id[:, None] * inv_freq[None, :]
    ang = jnp.concatenate([ang_r, ang_r, ang_c, ang_c], axis=-1)
    cos, sin = jnp.cos(ang), jnp.sin(ang)
    lane = np.arange(HEAD_DIM)
    sign = jnp.asarray(np.where((lane % (HEAD_DIM // 2)) < HEAD_DIM // 4, -1.0, 1.0), F32)
    sin_signed = sin * sign[None, :]
    scale = HEAD_DIM ** -0.5 * np.log2(np.e)
    return (jnp.tile(cos, (1, N_HEADS)) * scale, jnp.tile(sin_signed, (1, N_HEADS)) * scale,
            jnp.tile(cos, (1, N_KV_HEADS)), jnp.tile(sin_signed, (1, N_KV_HEADS)))


def _segment_matrix(width, seg_len):
    seg = np.arange(width) // seg_len
    return jnp.asarray((seg[:, None] == seg[None, :]).astype(np.float32), BF16)


def kernel(x, mix_norm_w, w_in, q_norm_w, k_norm_w, hgrn_lb_logits, hgrn_gnorm_w, conv_dw_w, conv_dw_b,
           conv_ln_w, conv_ln_b, conv_pw_w, conv_pw_b, attn_out_norm_w, conv_out_norm_w, w_out,
           ffn_norm_w, w_gate, w_up, w_down):
    batch, seq, d_model = x.shape
    assert d_model == D_MODEL and seq % TM_PROJ == 0 and seq % HG_CHUNK == 0
    assert seq % (ATTN_SUBTILES * TQ_ATTN) == 0 and seq % CONV_POST_ROWS == 0 and seq % HG_FINISH_ROWS == 0
    assert (batch * seq) % TM_FFN == 0 and w_in.shape == (DEPTH, D_MODEL, D_IN_PROJ)
    assert w_gate.shape == (DEPTH, D_MODEL, D_FF)

    tabs = _rope_tables(seq)
    segq = _segment_matrix(D_ATTN, HEAD_DIM)
    segk = _segment_matrix(D_KV, HEAD_DIM)
    segh = _segment_matrix(D_HGRN, HEAD_DIM)
    m2, lev, hm = _hgrn_constants()
    hconsts = (jnp.asarray(m2, BF16), jnp.asarray(lev), jnp.asarray(hm, BF16))
    logits = hgrn_lb_logits.astype(F32)

    x2 = x.reshape(batch * seq, D_MODEL)
    for l in range(DEPTH):
        q, kp, va, hg, cv = _inproj(
            x2, mix_norm_w[l][None, :], w_in[l].astype(BF16),
            jnp.tile(q_norm_w[l], N_HEADS)[None, :], jnp.tile(k_norm_w[l], N_KV_HEADS)[None, :],
            tabs, segq, segk, batch, seq)
        ya = _attention(q, kp, va, attn_out_norm_w[l][None, :], batch, seq)
        yh = _hgrn(hg, logits, jnp.tile(hgrn_gnorm_w[l], HGRN_HEADS)[None, :], hconsts, segh, l, batch, seq)
        yc = _conv(cv, conv_dw_w[l], conv_dw_b[l][None, :], conv_ln_w[l][None, :], conv_ln_b[l][None, :],
                   conv_pw_w[l].astype(BF16), conv_pw_b[l][None, :], conv_out_norm_w[l][None, :], batch, seq)
        x2 = _ffn(x2, ya, yh, yc, w_out[l].astype(BF16), ffn_norm_w[l][None, :],
                  w_gate[l].astype(BF16), w_up[l].astype(BF16), w_down[l].astype(BF16))
    return x2.reshape(batch, seq, D_MODEL)
```

```python
import functools

import numpy as np
import jax
import jax.numpy as jnp
from jax import lax
from jax.experimental import pallas as pl
from jax.experimental.pallas import tpu as pltpu

F32 = jnp.float32
BF16 = jnp.bfloat16

D_MODEL = 1024
DEPTH = 2
GRID_W = 64
D_ATTN = 512
D_HGRN = 256
D_CONV = 256
HEAD_DIM = 64
N_HEADS = D_ATTN // HEAD_DIM
N_KV_HEADS = 2
D_KV = N_KV_HEADS * HEAD_DIM
ROPE_THETA = 10000.0
HGRN_HEADS = D_HGRN // HEAD_DIM
F_MIN = 1e-6
LOG2_E = float(np.log2(np.e))
CONV_WIDTH = 31
CONV_PAD = (CONV_WIDTH - 1) // 2
D_FF = 2816
EPS = 1e-6
LN_EPS = 1e-5
D_IN_PROJ = D_ATTN + 2 * D_KV + 5 * D_HGRN + 2 * D_CONV

OFF_Q = 0
OFF_K = OFF_Q + D_ATTN
OFF_V = OFF_K + D_KV
OFF_H = OFF_V + D_KV
OFF_C = OFF_H + 5 * D_HGRN

LANES = 128
SUBLANES = 8
VMEM_LIMIT_BYTES = 56 * 1024 * 1024

TM_PROJ = 1024
TQ_ATTN = 256
ATTN_SUBTILES = 4
ATTN_AHEAD = 3
ATTN_SLOTS = ATTN_AHEAD + 1
TM_FFN = 1024
FF_CHUNK = 256
HG_CHUNK = 64
HG_FINE = 8
HG_REF = 2
HG_FINISH_ROWS = 256
HG_UNROLL = 4
CONV_ROWS = 64
CONV_POST_ROWS = 256
CONV_HALO = 16

assert -np.log(F_MIN) * (HG_FINE - 1 - HG_REF) < 80.0 and -np.log(F_MIN) * HG_REF < 80.0


def _params(n_grid_axes):
    return pltpu.CompilerParams(
        dimension_semantics=("arbitrary",) * n_grid_axes,
        vmem_limit_bytes=VMEM_LIMIT_BYTES)


def _full(shape):
    nd = len(shape)
    return pl.BlockSpec(shape, lambda *_: (0,) * nd)


def _segment_mean_sq(y, seg_ref, seg_len):
    return jnp.dot((y * y).astype(BF16), seg_ref[...], preferred_element_type=F32) * (1.0 / seg_len)


def _rope(xn, cos, sin_signed, first_mask):
    width = xn.shape[1]
    left = pltpu.roll(xn, width - HEAD_DIM // 4, 1)
    right = pltpu.roll(xn, HEAD_DIM // 4, 1)
    rot = jnp.where(first_mask, left, right)
    return xn * cos + rot * sin_signed


def _inproj_kernel(x_ref, nw_ref, w_ref, qnw_ref, knw_ref, cq_ref, sq_ref, ck_ref, sk_ref,
                   segq_ref, segk_ref,
                   q_out, kp_out, va_out, hg_out, cv_out):
    x = x_ref[...]
    ms = jnp.mean(x * x, axis=-1, keepdims=True)
    h = (x * lax.rsqrt(ms + EPS) * nw_ref[...]).astype(BF16)

    q = jnp.dot(h, w_ref[:, OFF_Q:OFF_K], preferred_element_type=F32)
    qn = q * lax.rsqrt(_segment_mean_sq(q, segq_ref, HEAD_DIM) + EPS) * qnw_ref[...]
    lane_q = lax.broadcasted_iota(jnp.int32, (1, D_ATTN), 1)
    first_q = (lane_q & (HEAD_DIM // 2 - 1)) < HEAD_DIM // 4
    q_out[...] = _rope(qn, cq_ref[...], sq_ref[...], first_q).T.astype(BF16)

    k = jnp.dot(h, w_ref[:, OFF_K:OFF_V], preferred_element_type=F32)
    kn = k * lax.rsqrt(_segment_mean_sq(k, segk_ref, HEAD_DIM) + EPS) * knw_ref[...]
    lane_k = lax.broadcasted_iota(jnp.int32, (1, D_KV), 1)
    first_k = (lane_k & (HEAD_DIM // 2 - 1)) < HEAD_DIM // 4
    kr = _rope(kn, ck_ref[...], sk_ref[...], first_k)
    ks = pltpu.roll(kr, HEAD_DIM, 1)
    low = lane_k < HEAD_DIM
    zero = jnp.zeros_like(kr)
    kp_out[0] = jnp.where(low, kr, zero).astype(BF16)
    kp_out[1] = jnp.where(low, zero, ks).astype(BF16)
    kp_out[2] = jnp.where(low, ks, zero).astype(BF16)
    kp_out[3] = jnp.where(low, zero, kr).astype(BF16)

    v = jnp.dot(h, w_ref[:, OFF_V:OFF_H], preferred_element_type=F32)
    va_out[...] = v.T.astype(BF16)

    hg_out[...] = jnp.dot(h, w_ref[:, OFF_H:OFF_C], preferred_element_type=F32)
    cv_out[...] = jnp.dot(h, w_ref[:, OFF_C:D_IN_PROJ], preferred_element_type=F32)


def _inproj(x2, nw, w_in, qnw, knw, tabs, segq, segk, batch, seq):
    tokens = x2.shape[0]
    n_seq_tiles = seq // TM_PROJ
    cq, sq, ck, sk = tabs
    tab_q = pl.BlockSpec((TM_PROJ, D_ATTN), lambda i: (i % n_seq_tiles, 0))
    tab_k = pl.BlockSpec((TM_PROJ, D_KV), lambda i: (i % n_seq_tiles, 0))
    row = lambda w: pl.BlockSpec((TM_PROJ, w), lambda i: (i, 0))
    return pl.pallas_call(
        _inproj_kernel,
        grid=(tokens // TM_PROJ,),
        in_specs=[row(D_MODEL), _full((1, D_MODEL)), _full((D_MODEL, D_IN_PROJ)),
                  _full((1, D_ATTN)), _full((1, D_KV)), tab_q, tab_q, tab_k, tab_k,
                  _full((D_ATTN, D_ATTN)), _full((D_KV, D_KV))],
        out_specs=[pl.BlockSpec((None, D_ATTN, TM_PROJ), lambda i: (i // n_seq_tiles, 0, i % n_seq_tiles)),
                   pl.BlockSpec((2 * N_KV_HEADS, TM_PROJ, D_KV), lambda i: (0, i, 0)),
                   pl.BlockSpec((None, D_KV, TM_PROJ), lambda i: (i // n_seq_tiles, 0, i % n_seq_tiles)),
                   row(5 * D_HGRN), row(2 * D_CONV)],
        out_shape=[jax.ShapeDtypeStruct((batch, D_ATTN, seq), BF16),
                   jax.ShapeDtypeStruct((2 * N_KV_HEADS, tokens, D_KV), BF16),
                   jax.ShapeDtypeStruct((batch, D_KV, seq), BF16),
                   jax.ShapeDtypeStruct((tokens, 5 * D_HGRN), F32),
                   jax.ShapeDtypeStruct((tokens, 2 * D_CONV), F32)],
        compiler_params=_params(1),
        name="inproj",
    )(x2, nw, w_in, qnw, knw, cq, sq, ck, sk, segq, segk)


def _attn_kernel(q_ref, kp_ref, va_ref, nw_ref, o_ref, st_ref, yt_ref):
    n_items = ATTN_SUBTILES * N_HEADS

    def scores_t(it):
        sub, hd = divmod(it, N_HEADS)
        g = hd // (N_HEADS // N_KV_HEADS)
        qp = q_ref[(hd // 2) * LANES:(hd // 2 + 1) * LANES, sub * TQ_ATTN:(sub + 1) * TQ_ATTN]
        st_ref[it % ATTN_SLOTS] = jnp.dot(kp_ref[2 * g + hd % 2], qp, preferred_element_type=F32)

    for it in range(ATTN_AHEAD):
        scores_t(it)
    for it in range(n_items):
        sub, hd = divmod(it, N_HEADS)
        g = hd // (N_HEADS // N_KV_HEADS)
        if it + ATTN_AHEAD < n_items:
            scores_t(it + ATTN_AHEAD)
        slot = it % ATTN_SLOTS
        m = jnp.max(st_ref[slot], axis=0, keepdims=True)
        p = jnp.exp2(st_ref[slot] - m)
        den = jnp.sum(p, axis=0, keepdims=True)
        num = jnp.dot(va_ref[g * HEAD_DIM:(g + 1) * HEAD_DIM, :], p.astype(BF16),
                      preferred_element_type=F32)
        yt_ref[hd * HEAD_DIM:(hd + 1) * HEAD_DIM, :] = num / den
        if hd == N_HEADS - 1:
            y = yt_ref[...].T
            ms = jnp.mean(y * y, axis=-1, keepdims=True)
            o_ref[sub * TQ_ATTN:(sub + 1) * TQ_ATTN, :] = (y * lax.rsqrt(ms + EPS) * nw_ref[...]).astype(BF16)


def _attention(q, kp, va, nw, batch, seq):
    tokens = batch * seq
    rows = ATTN_SUBTILES * TQ_ATTN
    nq = seq // rows
    return pl.pallas_call(
        _attn_kernel,
        grid=(batch, nq),
        in_specs=[pl.BlockSpec((None, D_ATTN, rows), lambda b, j: (b, 0, j)),
                  pl.BlockSpec((2 * N_KV_HEADS, seq, D_KV), lambda b, j: (0, b, 0)),
                  pl.BlockSpec((None, D_KV, seq), lambda b, j: (b, 0, 0)),
                  pl.BlockSpec((1, D_ATTN), lambda b, j: (0, 0))],
        out_specs=pl.BlockSpec((rows, D_ATTN), lambda b, j: (b * nq + j, 0)),
        out_shape=jax.ShapeDtypeStruct((tokens, D_ATTN), BF16),
        scratch_shapes=[pltpu.VMEM((ATTN_SLOTS, seq, TQ_ATTN), F32), pltpu.VMEM((D_ATTN, TQ_ATTN), F32)],
        compiler_params=_params(2),
        name="attn",
    )(q, kp, va, nw)


HG_LEVELS = (32, 16, 8)


def _hgrn_constants():
    n = HG_CHUNK
    idx = np.arange(n)
    fwd = (idx[None, :] <= idx[:, None]).astype(np.float32)
    bwd = fwd[::-1, ::-1]
    m2 = np.stack([np.tile(fwd, (1, 2)), np.tile(bwd, (1, 2))])

    lev = np.full((n, n), len(HG_LEVELS) + 1, np.int32)
    for t in range(n):
        for s in range(t + 1):
            if t // HG_FINE == s // HG_FINE:
                lev[t, s] = len(HG_LEVELS)
            else:
                for li, half in enumerate(HG_LEVELS):
                    if t // (2 * half) == s // (2 * half) and t // half != s // half:
                        lev[t, s] = li
                        break
    lev2 = np.stack([np.tile(lev, (1, HGRN_HEADS)), np.tile(lev[::-1, ::-1], (1, HGRN_HEADS))])
    head = np.arange(D_HGRN) // HEAD_DIM
    headmask = (head[:, None] == head[None, :]).astype(np.float32)
    return m2, lev2, headmask


def _hgrn_kernel(hq_ref, zf_ref, zb_ref, hi_ref, gate_ref, logit_ref, gnw_ref, m2_ref, lev_ref,
                 hmb_ref, seg_ref, o_ref, ofw_ref, obw_ref, st_ref, *, layer):
    seq = hq_ref.shape[0]
    n_chunks = seq // HG_CHUNK
    n = HG_CHUNK

    lg = [logit_ref[i] for i in range(DEPTH)]
    mx = functools.reduce(jnp.maximum, lg)
    ex = [jnp.exp(v - mx) for v in lg]
    den = functools.reduce(lambda a, b: a + b, ex)
    sm = [e / den for e in ex]
    acc = sm[0]
    for i in range(1, layer + 1):
        acc = acc + sm[i]
    lb = acc - sm[0]

    def stage_a(d, z_ref, lb_row, r0):
        z = z_ref[pl.ds(r0, n), :]
        f = lb_row + (1.0 - lb_row) * jax.nn.sigmoid(z)
        g = jnp.log(jnp.maximum(f, F_MIN)) * LOG2_E
        kk = 1.0 - f
        g1 = g.astype(BF16)
        g2 = (g - g1.astype(F32)).astype(BF16)
        gcat = jnp.concatenate([g1, g2], axis=0)
        e_cum = jnp.dot(m2_ref[d], gcat, preferred_element_type=F32)
        return kk, e_cum

    def level_exponent(e_cum, d, li):
        fine = li == len(HG_LEVELS)
        blk = HG_FINE if fine else HG_LEVELS[li]
        parts = []
        for r in range(0, n, blk):
            p_start = r if d == 0 else n - blk - r
            if fine:
                p_ref, later = p_start + HG_REF, True
            else:
                p_ref = (p_start // (2 * blk)) * 2 * blk + blk
                later = p_start >= p_ref
            t_ref = p_ref if d == 0 else n - 1 - p_ref
            ref = e_cum[t_ref:t_ref + 1]
            rows = e_cum[r:r + blk]
            parts.append(rows - ref if later else ref - rows)
        return jnp.concatenate(parts, axis=0)

    def stage_b(d, r0, kk, e_cum):
        hm_bf = hmb_ref[...]
        q = hq_ref[pl.ds(r0, n), :]
        v = hi_ref[pl.ds(r0, n), :]
        lev = lev_ref[d]
        a = jnp.zeros((n, D_HGRN), F32)
        for li in range(len(HG_LEVELS) + 1):
            e_lev = level_exponent(e_cum, d, li)
            x_q = jnp.exp2(e_lev)
            x_k = x_q if li < len(HG_LEVELS) else jnp.exp2(-e_lev)
            qt = (q * x_q).astype(BF16)
            kt = (kk * x_k).astype(BF16)
            kbd = jnp.concatenate([kt] * HGRN_HEADS, axis=0) * hm_bf
            sc = lax.dot_general(qt, kbd, (((1,), (1,)), ((), ())), preferred_element_type=F32)
            a = jnp.where(lev == li, sc, a)
        return a.astype(BF16)

    def stage_b2(d, r0, kk, e_cum, a):
        hm_bf = hmb_ref[...]
        q = hq_ref[pl.ds(r0, n), :]
        v = hi_ref[pl.ds(r0, n), :]
        vbd = jnp.concatenate([v.astype(BF16)] * HGRN_HEADS, axis=0) * hm_bf
        o_intra = jnp.dot(a, vbd, preferred_element_type=F32)

        e_tot = e_cum[n - 1:n] if d == 0 else e_cum[0:1]
        qb = (q * jnp.exp2(e_cum)).astype(BF16)
        kdec = (kk * jnp.exp2(e_tot - e_cum)).astype(BF16)
        zpad = jnp.zeros((LANES - n, D_HGRN), F32)
        vt = jnp.concatenate([v, zpad], axis=0).T.astype(BF16)
        kpad = jnp.concatenate([kdec, jnp.zeros((LANES - n, D_HGRN), BF16)], axis=0)
        upd = jnp.dot(vt, kpad, preferred_element_type=F32)
        return o_intra, qb, upd, jnp.exp2(e_tot)

    def stage_c(d, o_intra, qb, upd, decay_tot):
        st = st_ref[d]
        o = o_intra + lax.dot_general(qb, st.astype(BF16) * hmb_ref[...], (((1,), (1,)), ((), ())),
                                      preferred_element_type=F32)
        st_ref[d] = st * decay_tot + upd
        return o

    st_ref[...] = jnp.zeros_like(st_ref)

    def scan_body(ci, carry):
        items = []
        for u in range(HG_UNROLL):
            cf = ci * HG_UNROLL + u
            items.append((0, zf_ref, lb[0:1], pl.multiple_of(cf * n, n), ofw_ref))
            items.append((1, zb_ref, lb[1:2], pl.multiple_of((n_chunks - 1 - cf) * n, n), obw_ref))
        sa = [stage_a(d, z_ref, lb_row, r0) for d, z_ref, lb_row, r0, _ in items]
        sb = [stage_b(d, r0, *a) for (d, _, _, r0, _), a in zip(items, sa)]
        sb2 = [stage_b2(d, r0, *a, b) for (d, _, _, r0, _), a, b in zip(items, sa, sb)]
        for (d, _, _, r0, out_ref), b in zip(items, sb2):
            out_ref[pl.ds(r0, n), :] = stage_c(d, *b)
        return carry

    lax.fori_loop(0, n_chunks // HG_UNROLL, scan_body, 0)

    gnw = gnw_ref[...]

    def finish_body(i, carry):
        r0 = pl.multiple_of(i * HG_FINISH_ROWS, HG_FINISH_ROWS)
        o = ofw_ref[pl.ds(r0, HG_FINISH_ROWS), :] + obw_ref[pl.ds(r0, HG_FINISH_ROWS), :]
        on = o * lax.rsqrt(_segment_mean_sq(o, seg_ref, HEAD_DIM) + EPS) * gnw
        gate = gate_ref[pl.ds(r0, HG_FINISH_ROWS), :]
        o_ref[pl.ds(r0, HG_FINISH_ROWS), :] = (on * (gate * jax.nn.sigmoid(gate))).astype(BF16)
        return carry

    lax.fori_loop(0, seq // HG_FINISH_ROWS, finish_body, 0)


def _hgrn(hg, logits, gnw, consts, seg, layer, batch, seq):
    tokens = hg.shape[0]
    m2, lev, hmb = consts
    col = lambda j: pl.BlockSpec((seq, D_HGRN), lambda b: (b, j))
    return pl.pallas_call(
        functools.partial(_hgrn_kernel, layer=layer),
        grid=(batch,),
        in_specs=[col(0), col(1), col(2), col(3), col(4),
                  _full(logits.shape), _full((1, D_HGRN)), _full(m2.shape), _full(lev.shape),
                  _full(hmb.shape), _full(seg.shape)],
        out_specs=pl.BlockSpec((seq, D_HGRN), lambda b: (b, 0)),
        out_shape=jax.ShapeDtypeStruct((tokens, D_HGRN), BF16),
        scratch_shapes=[pltpu.VMEM((seq, D_HGRN), F32), pltpu.VMEM((seq, D_HGRN), F32),
                        pltpu.VMEM((2, D_HGRN, D_HGRN), F32)],
        compiler_params=_params(1),
        name="hgrn",
    )(hg, hg, hg, hg, hg, logits, gnw, m2, lev, hmb, seg)


def _conv_kernel(a_ref, b_ref, dww_ref, dwb_ref, lnw_ref, lnb_ref, pww_ref, pwb_ref, nw_ref,
                 o_ref, pad_ref, dw_ref):
    seq = a_ref.shape[0]
    halo = jnp.zeros((CONV_HALO, D_CONV), F32)
    pad_ref[0:CONV_HALO, :] = halo
    pad_ref[CONV_HALO + seq:CONV_HALO + seq + CONV_HALO, :] = halo
    pad_ref[CONV_HALO:CONV_HALO + seq, :] = a_ref[...] * jax.nn.sigmoid(b_ref[...])

    win_rows = CONV_ROWS + 2 * CONV_HALO

    def tap_body(i, carry):
        r0 = pl.multiple_of(i * CONV_ROWS, CONV_ROWS)
        win = pad_ref[pl.ds(r0, win_rows), :]
        acc = jnp.zeros((CONV_ROWS, D_CONV), F32)
        for r in range(SUBLANES):
            shifted = win if r == 0 else pltpu.roll(win, win_rows - r, 0)
            for a8 in range(0, 2 * CONV_HALO, SUBLANES):
                tap = a8 + r - (CONV_HALO - CONV_PAD)
                if 0 <= tap < CONV_WIDTH:
                    acc = acc + shifted[a8:a8 + CONV_ROWS] * dww_ref[tap:tap + 1, :]
        dw_ref[pl.ds(r0, CONV_ROWS), :] = acc + dwb_ref[...]
        return carry

    lax.fori_loop(0, seq // CONV_ROWS, tap_body, 0, unroll=2)

    def post_body(i, carry):
        r0 = pl.multiple_of(i * CONV_POST_ROWS, CONV_POST_ROWS)
        u = dw_ref[pl.ds(r0, CONV_POST_ROWS), :]
        mu = jnp.mean(u, axis=-1, keepdims=True)
        uc = u - mu
        var = jnp.mean(uc * uc, axis=-1, keepdims=True)
        y = uc * lax.rsqrt(var + LN_EPS) * lnw_ref[...] + lnb_ref[...]
        y = y * jax.nn.sigmoid(y)
        y = jnp.dot(y.astype(BF16), pww_ref[...], preferred_element_type=F32) + pwb_ref[...]
        ms = jnp.mean(y * y, axis=-1, keepdims=True)
        o_ref[pl.ds(r0, CONV_POST_ROWS), :] = (y * lax.rsqrt(ms + EPS) * nw_ref[...]).astype(BF16)
        return carry

    lax.fori_loop(0, seq // CONV_POST_ROWS, post_body, 0, unroll=2)


def _conv(cv, dww, dwb, lnw, lnb, pww, pwb, nw, batch, seq):
    tokens = cv.shape[0]
    col = lambda j: pl.BlockSpec((seq, D_CONV), lambda b: (b, j))
    vec = _full((1, D_CONV))
    return pl.pallas_call(
        _conv_kernel,
        grid=(batch,),
        in_specs=[col(0), col(1), _full(dww.shape), vec, vec, vec, _full((D_CONV, D_CONV)), vec, vec],
        out_specs=pl.BlockSpec((seq, D_CONV), lambda b: (b, 0)),
        out_shape=jax.ShapeDtypeStruct((tokens, D_CONV), BF16),
        scratch_shapes=[pltpu.VMEM((seq + 2 * CONV_HALO, D_CONV), F32), pltpu.VMEM((seq, D_CONV), F32)],
        compiler_params=_params(1),
        name="conv",
    )(cv, cv, dww, dwb, lnw, lnb, pww, pwb, nw)


def _ffn_kernel(x_ref, ya_ref, yh_ref, yc_ref, wo_ref, nw_ref, wg_ref, wu_ref, wd_ref,
                o_ref, act_ref):
    x1 = x_ref[...]
    x1 = x1 + jnp.dot(ya_ref[...], wo_ref[0:D_ATTN, :], preferred_element_type=F32)
    x1 = x1 + jnp.dot(yh_ref[...], wo_ref[D_ATTN:D_ATTN + D_HGRN, :], preferred_element_type=F32)
    x1 = x1 + jnp.dot(yc_ref[...], wo_ref[D_ATTN + D_HGRN:D_MODEL, :], preferred_element_type=F32)
    ms = jnp.mean(x1 * x1, axis=-1, keepdims=True)
    h = (x1 * lax.rsqrt(ms + EPS) * nw_ref[...]).astype(BF16)
    for c in range(0, D_FF, FF_CHUNK):
        g = jnp.dot(h, wg_ref[:, c:c + FF_CHUNK], preferred_element_type=F32)
        u = jnp.dot(h, wu_ref[:, c:c + FF_CHUNK], preferred_element_type=F32)
        act_ref[:, c:c + FF_CHUNK] = (g * jax.nn.sigmoid(g) * u).astype(BF16)
    o_ref[...] = x1 + jnp.dot(act_ref[...], wd_ref[...], preferred_element_type=F32)


def _ffn(x2, ya, yh, yc, wo, nw, wg, wu, wd):
    tokens = x2.shape[0]
    row = lambda w: pl.BlockSpec((TM_FFN, w), lambda i: (i, 0))
    resident = lambda shape: pl.BlockSpec(shape, lambda i: (0, 0), pipeline_mode=pl.Buffered(1))
    return pl.pallas_call(
        _ffn_kernel,
        grid=(tokens // TM_FFN,),
        in_specs=[row(D_MODEL), row(D_ATTN), row(D_HGRN), row(D_CONV),
                  resident((D_MODEL, D_MODEL)), _full((1, D_MODEL)),
                  resident((D_MODEL, D_FF)), resident((D_MODEL, D_FF)), resident((D_FF, D_MODEL))],
        out_specs=row(D_MODEL),
        out_shape=jax.ShapeDtypeStruct((tokens, D_MODEL), F32),
        scratch_shapes=[pltpu.VMEM((TM_FFN, D_FF), BF16)],
        compiler_params=_params(1),
        name="ffn",
    )(x2, ya, yh, yc, wo, nw, wg, wu, wd)


def _rope_tables(seq):
    rows = seq // GRID_W
    row_id = jnp.repeat(jnp.arange(rows, dtype=F32), GRID_W)
    col_id = jnp.tile(jnp.arange(GRID_W, dtype=F32), rows)
    half = HEAD_DIM // 2
    inv_freq = ROPE_THETA ** (-jnp.arange(0, half, 2, dtype=F32) / half)
    ang_r = row_id[:, None] * inv_freq[None, :]
    ang_c = col_id[:, None] * inv_freq[None, :]
    ang = jnp.concatenate([ang_r, ang_r, ang_c, ang_c], axis=-1)
    cos, sin = jnp.cos(ang), jnp.sin(ang)
    lane = np.arange(HEAD_DIM)
    sign = jnp.asarray(np.where((lane % (HEAD_DIM // 2)) < HEAD_DIM // 4, -1.0, 1.0), F32)
    sin_signed = sin * sign[None, :]
    scale = HEAD_DIM ** -0.5 * np.log2(np.e)
    return (jnp.tile(cos, (1, N_HEADS)) * scale, jnp.tile(sin_signed, (1, N_HEADS)) * scale,
            jnp.tile(cos, (1, N_KV_HEADS)), jnp.tile(sin_signed, (1, N_KV_HEADS)))


def _segment_matrix(width, seg_len):
    seg = np.arange(width) // seg_len
    return jnp.asarray((seg[:, None] == seg[None, :]).astype(np.float32), BF16)


def kernel(x, mix_norm_w, w_in, q_norm_w, k_norm_w, hgrn_lb_logits, hgrn_gnorm_w, conv_dw_w, conv_dw_b,
           conv_ln_w, conv_ln_b, conv_pw_w, conv_pw_b, attn_out_norm_w, conv_out_norm_w, w_out,
           ffn_norm_w, w_gate, w_up, w_down):
    batch, seq, d_model = x.shape
    assert d_model == D_MODEL and seq % TM_PROJ == 0 and seq % HG_CHUNK == 0
    assert seq % (ATTN_SUBTILES * TQ_ATTN) == 0 and seq % CONV_POST_ROWS == 0 and seq % HG_FINISH_ROWS == 0
    assert (batch * seq) % TM_FFN == 0 and w_in.shape == (DEPTH, D_MODEL, D_IN_PROJ)
    assert w_gate.shape == (DEPTH, D_MODEL, D_FF)

    tabs = _rope_tables(seq)
    segq = _segment_matrix(D_ATTN, HEAD_DIM)
    segk = _segment_matrix(D_KV, HEAD_DIM)
    segh = _segment_matrix(D_HGRN, HEAD_DIM)
    m2, lev, hm = _hgrn_constants()
    hconsts = (jnp.asarray(m2, BF16), jnp.asarray(lev), jnp.asarray(hm, BF16))
    logits = hgrn_lb_logits.astype(F32)

    x2 = x.reshape(batch * seq, D_MODEL)
    for l in range(DEPTH):
        q, kp, va, hg, cv = _inproj(
            x2, mix_norm_w[l][None, :], w_in[l].astype(BF16),
            jnp.tile(q_norm_w[l], N_HEADS)[None, :], jnp.tile(k_norm_w[l], N_KV_HEADS)[None, :],
            tabs, segq, segk, batch, seq)
        ya = _attention(q, kp, va, attn_out_norm_w[l][None, :], batch, seq)
        yh = _hgrn(hg, logits, jnp.tile(hgrn_gnorm_w[l], HGRN_HEADS)[None, :], hconsts, segh, l, batch, seq)
        yc = _conv(cv, conv_dw_w[l], conv_dw_b[l][None, :], conv_ln_w[l][None, :], conv_ln_b[l][None, :],
                   conv_pw_w[l].astype(BF16), conv_pw_b[l][None, :], conv_out_norm_w[l][None, :], batch, seq)
        x2 = _ffn(x2, ya, yh, yc, w_out[l].astype(BF16), ffn_norm_w[l][None, :],
                  w_gate[l].astype(BF16), w_up[l].astype(BF16), w_down[l].astype(BF16))
    return x2.reshape(batch, seq, D_MODEL)
```

```python
import functools

import numpy as np
import jax
import jax.numpy as jnp
from jax import lax
from jax.experimental import pallas as pl
from jax.experimental.pallas import tpu as pltpu

F32 = jnp.float32
BF16 = jnp.bfloat16

D_MODEL = 1024
DEPTH = 2
GRID_W = 64
D_ATTN = 512
D_HGRN = 256
D_CONV = 256
HEAD_DIM = 64
N_HEADS = D_ATTN // HEAD_DIM
N_KV_HEADS = 2
D_KV = N_KV_HEADS * HEAD_DIM
ROPE_THETA = 10000.0
HGRN_HEADS = D_HGRN // HEAD_DIM
F_MIN = 1e-6
LOG2_E = float(np.log2(np.e))
CONV_WIDTH = 31
CONV_PAD = (CONV_WIDTH - 1) // 2
D_FF = 2816
EPS = 1e-6
LN_EPS = 1e-5
D_IN_PROJ = D_ATTN + 2 * D_KV + 5 * D_HGRN + 2 * D_CONV

OFF_Q = 0
OFF_K = OFF_Q + D_ATTN
OFF_V = OFF_K + D_KV
OFF_H = OFF_V + D_KV
OFF_C = OFF_H + 5 * D_HGRN

LANES = 128
SUBLANES = 8
VMEM_LIMIT_BYTES = 56 * 1024 * 1024

TM_PROJ = 1024
TQ_ATTN = 256
ATTN_SUBTILES = 4
ATTN_AHEAD = 3
ATTN_SLOTS = ATTN_AHEAD + 1
TM_FFN = 1024
FF_CHUNK = 256
HG_CHUNK = 64
HG_FINE = 8
HG_REF = 2
HG_FINISH_ROWS = 256
HG_UNROLL = 4
CONV_ROWS = 128
CONV_POST_ROWS = 256
CONV_HALO = 16

assert -np.log(F_MIN) * (HG_FINE - 1 - HG_REF) < 80.0 and -np.log(F_MIN) * HG_REF < 80.0


def _params(n_grid_axes):
    return pltpu.CompilerParams(
        dimension_semantics=("arbitrary",) * n_grid_axes,
        vmem_limit_bytes=VMEM_LIMIT_BYTES)


def _full(shape):
    nd = len(shape)
    return pl.BlockSpec(shape, lambda *_: (0,) * nd)


def _segment_mean_sq(y, seg_ref, seg_len):
    return jnp.dot((y * y).astype(BF16), seg_ref[...], preferred_element_type=F32) * (1.0 / seg_len)


def _rope(xn, cos, sin_signed, first_mask):
    width = xn.shape[1]
    left = pltpu.roll(xn, width - HEAD_DIM // 4, 1)
    right = pltpu.roll(xn, HEAD_DIM // 4, 1)
    rot = jnp.where(first_mask, left, right)
    return xn * cos + rot * sin_signed


def _inproj_kernel(x_ref, nw_ref, w_ref, qnw_ref, knw_ref, cq_ref, sq_ref, ck_ref, sk_ref,
                   segq_ref, segk_ref,
                   q_out, kp_out, va_out, hg_out, cv_out):
    x = x_ref[...]
    ms = jnp.mean(x * x, axis=-1, keepdims=True)
    h = (x * lax.rsqrt(ms + EPS) * nw_ref[...]).astype(BF16)

    q = jnp.dot(h, w_ref[:, OFF_Q:OFF_K], preferred_element_type=F32)
    qn = q * lax.rsqrt(_segment_mean_sq(q, segq_ref, HEAD_DIM) + EPS) * qnw_ref[...]
    lane_q = lax.broadcasted_iota(jnp.int32, (1, D_ATTN), 1)
    first_q = (lane_q & (HEAD_DIM // 2 - 1)) < HEAD_DIM // 4
    q_out[...] = _rope(qn, cq_ref[...], sq_ref[...], first_q).T.astype(BF16)

    k = jnp.dot(h, w_ref[:, OFF_K:OFF_V], preferred_element_type=F32)
    kn = k * lax.rsqrt(_segment_mean_sq(k, segk_ref, HEAD_DIM) + EPS) * knw_ref[...]
    lane_k = lax.broadcasted_iota(jnp.int32, (1, D_KV), 1)
    first_k = (lane_k & (HEAD_DIM // 2 - 1)) < HEAD_DIM // 4
    kr = _rope(kn, ck_ref[...], sk_ref[...], first_k)
    ks = pltpu.roll(kr, HEAD_DIM, 1)
    low = lane_k < HEAD_DIM
    zero = jnp.zeros_like(kr)
    kp_out[0] = jnp.where(low, kr, zero).astype(BF16)
    kp_out[1] = jnp.where(low, zero, ks).astype(BF16)
    kp_out[2] = jnp.where(low, ks, zero).astype(BF16)
    kp_out[3] = jnp.where(low, zero, kr).astype(BF16)

    v = jnp.dot(h, w_ref[:, OFF_V:OFF_H], preferred_element_type=F32)
    va_out[...] = v.T.astype(BF16)

    hg_out[...] = jnp.dot(h, w_ref[:, OFF_H:OFF_C], preferred_element_type=F32)
    cv_out[...] = jnp.dot(h, w_ref[:, OFF_C:D_IN_PROJ], preferred_element_type=F32)


def _inproj(x2, nw, w_in, qnw, knw, tabs, segq, segk, batch, seq):
    tokens = x2.shape[0]
    n_seq_tiles = seq // TM_PROJ
    cq, sq, ck, sk = tabs
    tab_q = pl.BlockSpec((TM_PROJ, D_ATTN), lambda i: (i % n_seq_tiles, 0))
    tab_k = pl.BlockSpec((TM_PROJ, D_KV), lambda i: (i % n_seq_tiles, 0))
    row = lambda w: pl.BlockSpec((TM_PROJ, w), lambda i: (i, 0))
    return pl.pallas_call(
        _inproj_kernel,
        grid=(tokens // TM_PROJ,),
        in_specs=[row(D_MODEL), _full((1, D_MODEL)), _full((D_MODEL, D_IN_PROJ)),
                  _full((1, D_ATTN)), _full((1, D_KV)), tab_q, tab_q, tab_k, tab_k,
                  _full((D_ATTN, D_ATTN)), _full((D_KV, D_KV))],
        out_specs=[pl.BlockSpec((None, D_ATTN, TM_PROJ), lambda i: (i // n_seq_tiles, 0, i % n_seq_tiles)),
                   pl.BlockSpec((2 * N_KV_HEADS, TM_PROJ, D_KV), lambda i: (0, i, 0)),
                   pl.BlockSpec((None, D_KV, TM_PROJ), lambda i: (i // n_seq_tiles, 0, i % n_seq_tiles)),
                   row(5 * D_HGRN), row(2 * D_CONV)],
        out_shape=[jax.ShapeDtypeStruct((batch, D_ATTN, seq), BF16),
                   jax.ShapeDtypeStruct((2 * N_KV_HEADS, tokens, D_KV), BF16),
                   jax.ShapeDtypeStruct((batch, D_KV, seq), BF16),
                   jax.ShapeDtypeStruct((tokens, 5 * D_HGRN), F32),
                   jax.ShapeDtypeStruct((tokens, 2 * D_CONV), F32)],
        compiler_params=_params(1),
        name="inproj",
    )(x2, nw, w_in, qnw, knw, cq, sq, ck, sk, segq, segk)


def _attn_kernel(q_ref, kp_ref, va_ref, nw_ref, o_ref, st_ref, yt_ref):
    n_items = ATTN_SUBTILES * N_HEADS

    def scores_t(it):
        sub, hd = divmod(it, N_HEADS)
        g = hd // (N_HEADS // N_KV_HEADS)
        qp = q_ref[(hd // 2) * LANES:(hd // 2 + 1) * LANES, sub * TQ_ATTN:(sub + 1) * TQ_ATTN]
        st_ref[it % ATTN_SLOTS] = jnp.dot(kp_ref[2 * g + hd % 2], qp, preferred_element_type=F32)

    for it in range(ATTN_AHEAD):
        scores_t(it)
    for it in range(n_items):
        sub, hd = divmod(it, N_HEADS)
        g = hd // (N_HEADS // N_KV_HEADS)
        if it + ATTN_AHEAD < n_items:
            scores_t(it + ATTN_AHEAD)
        slot = it % ATTN_SLOTS
        m = jnp.max(st_ref[slot], axis=0, keepdims=True)
        p = jnp.exp2(st_ref[slot] - m)
        den = jnp.sum(p, axis=0, keepdims=True)
        num = jnp.dot(va_ref[g * HEAD_DIM:(g + 1) * HEAD_DIM, :], p.astype(BF16),
                      preferred_element_type=F32)
        yt_ref[hd * HEAD_DIM:(hd + 1) * HEAD_DIM, :] = num / den
        if hd == N_HEADS - 1:
            y = yt_ref[...].T
            ms = jnp.mean(y * y, axis=-1, keepdims=True)
            o_ref[sub * TQ_ATTN:(sub + 1) * TQ_ATTN, :] = (y * lax.rsqrt(ms + EPS) * nw_ref[...]).astype(BF16)


def _attention(q, kp, va, nw, batch, seq):
    tokens = batch * seq
    rows = ATTN_SUBTILES * TQ_ATTN
    nq = seq // rows
    return pl.pallas_call(
        _attn_kernel,
        grid=(batch, nq),
        in_specs=[pl.BlockSpec((None, D_ATTN, rows), lambda b, j: (b, 0, j)),
                  pl.BlockSpec((2 * N_KV_HEADS, seq, D_KV), lambda b, j: (0, b, 0)),
                  pl.BlockSpec((None, D_KV, seq), lambda b, j: (b, 0, 0)),
                  pl.BlockSpec((1, D_ATTN), lambda b, j: (0, 0))],
        out_specs=pl.BlockSpec((rows, D_ATTN), lambda b, j: (b * nq + j, 0)),
        out_shape=jax.ShapeDtypeStruct((tokens, D_ATTN), BF16),
        scratch_shapes=[pltpu.VMEM((ATTN_SLOTS, seq, TQ_ATTN), F32), pltpu.VMEM((D_ATTN, TQ_ATTN), F32)],
        compiler_params=_params(2),
        name="attn",
    )(q, kp, va, nw)


HG_LEVELS = (32, 16, 8)


def _hgrn_constants():
    n = HG_CHUNK
    idx = np.arange(n)
    fwd = (idx[None, :] <= idx[:, None]).astype(np.float32)
    bwd = fwd[::-1, ::-1]
    m2 = np.stack([np.tile(fwd, (1, 2)), np.tile(bwd, (1, 2))])

    lev = np.full((n, n), len(HG_LEVELS) + 1, np.int32)
    for t in range(n):
        for s in range(t + 1):
            if t // HG_FINE == s // HG_FINE:
                lev[t, s] = len(HG_LEVELS)
            else:
                for li, half in enumerate(HG_LEVELS):
                    if t // (2 * half) == s // (2 * half) and t // half != s // half:
                        lev[t, s] = li
                        break
    lev2 = np.stack([np.tile(lev, (1, HGRN_HEADS)), np.tile(lev[::-1, ::-1], (1, HGRN_HEADS))])
    head = np.arange(D_HGRN) // HEAD_DIM
    headmask = (head[:, None] == head[None, :]).astype(np.float32)
    return m2, lev2, headmask


def _hgrn_kernel(hq_ref, zf_ref, zb_ref, hi_ref, gate_ref, logit_ref, gnw_ref, m2_ref, lev_ref,
                 hmb_ref, seg_ref, o_ref, ofw_ref, obw_ref, st_ref, *, layer):
    seq = hq_ref.shape[0]
    n_chunks = seq // HG_CHUNK
    n = HG_CHUNK

    lg = [logit_ref[i] for i in range(DEPTH)]
    mx = functools.reduce(jnp.maximum, lg)
    ex = [jnp.exp(v - mx) for v in lg]
    den = functools.reduce(lambda a, b: a + b, ex)
    sm = [e / den for e in ex]
    acc = sm[0]
    for i in range(1, layer + 1):
        acc = acc + sm[i]
    lb = acc - sm[0]

    def stage_a(d, z_ref, lb_row, r0):
        z = z_ref[pl.ds(r0, n), :]
        f = lb_row + (1.0 - lb_row) * jax.nn.sigmoid(z)
        g = jnp.log(jnp.maximum(f, F_MIN)) * LOG2_E
        kk = 1.0 - f
        g1 = g.astype(BF16)
        g2 = (g - g1.astype(F32)).astype(BF16)
        gcat = jnp.concatenate([g1, g2], axis=0)
        e_cum = jnp.dot(m2_ref[d], gcat, preferred_element_type=F32)
        return kk, e_cum

    def level_exponent(e_cum, d, li):
        fine = li == len(HG_LEVELS)
        blk = HG_FINE if fine else HG_LEVELS[li]
        parts = []
        for r in range(0, n, blk):
            p_start = r if d == 0 else n - blk - r
            if fine:
                p_ref, later = p_start + HG_REF, True
            else:
                p_ref = (p_start // (2 * blk)) * 2 * blk + blk
                later = p_start >= p_ref
            t_ref = p_ref if d == 0 else n - 1 - p_ref
            ref = e_cum[t_ref:t_ref + 1]
            rows = e_cum[r:r + blk]
            parts.append(rows - ref if later else ref - rows)
        return jnp.concatenate(parts, axis=0)

    def stage_b(d, r0, kk, e_cum):
        hm_bf = hmb_ref[...]
        q = hq_ref[pl.ds(r0, n), :]
        v = hi_ref[pl.ds(r0, n), :]
        lev = lev_ref[d]
        a = jnp.zeros((n, D_HGRN), F32)
        for li in range(len(HG_LEVELS) + 1):
            e_lev = level_exponent(e_cum, d, li)
            x_q = jnp.exp2(e_lev)
            x_k = x_q if li < len(HG_LEVELS) else jnp.exp2(-e_lev)
            qt = (q * x_q).astype(BF16)
            kt = (kk * x_k).astype(BF16)
            kbd = jnp.concatenate([kt] * HGRN_HEADS, axis=0) * hm_bf
            sc = lax.dot_general(qt, kbd, (((1,), (1,)), ((), ())), preferred_element_type=F32)
            a = jnp.where(lev == li, sc, a)
        return a.astype(BF16)

    def stage_b2(d, r0, kk, e_cum, a):
        hm_bf = hmb_ref[...]
        q = hq_ref[pl.ds(r0, n), :]
        v = hi_ref[pl.ds(r0, n), :]
        vbd = jnp.concatenate([v.astype(BF16)] * HGRN_HEADS, axis=0) * hm_bf
        o_intra = jnp.dot(a, vbd, preferred_element_type=F32)

        e_tot = e_cum[n - 1:n] if d == 0 else e_cum[0:1]
        qb = (q * jnp.exp2(e_cum)).astype(BF16)
        kdec = kk * jnp.exp2(e_tot - e_cum)
        zpad = jnp.zeros((LANES - n - SUBLANES, D_HGRN), F32)
        tot8 = jnp.broadcast_to(e_tot, (SUBLANES, D_HGRN))
        kt = jnp.concatenate([kdec, tot8, zpad], axis=0).T
        decay_col = jnp.exp2(kt[:, n:n + 1])
        vpad = jnp.concatenate([v.astype(BF16), jnp.zeros((LANES - n, D_HGRN), BF16)], axis=0)
        upd = jnp.dot(kt.astype(BF16), vpad, preferred_element_type=F32)
        return o_intra, qb, upd, decay_col

    def stage_c(d, o_intra, qb, upd, decay_col):
        st = st_ref[d]
        o = o_intra + jnp.dot(qb, st.astype(BF16) * hmb_ref[...], preferred_element_type=F32)
        st_ref[d] = st * decay_col + upd
        return o

    st_ref[...] = jnp.zeros_like(st_ref)

    def scan_body(ci, carry):
        items = []
        for u in range(HG_UNROLL):
            cf = ci * HG_UNROLL + u
            items.append((0, zf_ref, lb[0:1], pl.multiple_of(cf * n, n), ofw_ref))
            items.append((1, zb_ref, lb[1:2], pl.multiple_of((n_chunks - 1 - cf) * n, n), obw_ref))
        sa = [stage_a(d, z_ref, lb_row, r0) for d, z_ref, lb_row, r0, _ in items]
        sb = [stage_b(d, r0, *a) for (d, _, _, r0, _), a in zip(items, sa)]
        sb2 = [stage_b2(d, r0, *a, b) for (d, _, _, r0, _), a, b in zip(items, sa, sb)]
        for (d, _, _, r0, out_ref), b in zip(items, sb2):
            out_ref[pl.ds(r0, n), :] = stage_c(d, *b)
        return carry

    lax.fori_loop(0, n_chunks // HG_UNROLL, scan_body, 0)

    gnw = gnw_ref[...]

    def finish_body(i, carry):
        r0 = pl.multiple_of(i * HG_FINISH_ROWS, HG_FINISH_ROWS)
        o = ofw_ref[pl.ds(r0, HG_FINISH_ROWS), :] + obw_ref[pl.ds(r0, HG_FINISH_ROWS), :]
        on = o * lax.rsqrt(_segment_mean_sq(o, seg_ref, HEAD_DIM) + EPS) * gnw
        gate = gate_ref[pl.ds(r0, HG_FINISH_ROWS), :]
        o_ref[pl.ds(r0, HG_FINISH_ROWS), :] = (on * (gate * jax.nn.sigmoid(gate))).astype(BF16)
        return carry

    lax.fori_loop(0, seq // HG_FINISH_ROWS, finish_body, 0)


def _hgrn(hg, logits, gnw, consts, seg, layer, batch, seq):
    tokens = hg.shape[0]
    m2, lev, hmb = consts
    col = lambda j: pl.BlockSpec((seq, D_HGRN), lambda b: (b, j))
    return pl.pallas_call(
        functools.partial(_hgrn_kernel, layer=layer),
        grid=(batch,),
        in_specs=[col(0), col(1), col(2), col(3), col(4),
                  _full(logits.shape), _full((1, D_HGRN)), _full(m2.shape), _full(lev.shape),
                  _full(hmb.shape), _full(seg.shape)],
        out_specs=pl.BlockSpec((seq, D_HGRN), lambda b: (b, 0)),
        out_shape=jax.ShapeDtypeStruct((tokens, D_HGRN), BF16),
        scratch_shapes=[pltpu.VMEM((seq, D_HGRN), F32), pltpu.VMEM((seq, D_HGRN), F32),
                        pltpu.VMEM((2, D_HGRN, D_HGRN), F32)],
        compiler_params=_params(1),
        name="hgrn",
    )(hg, hg, hg, hg, hg, logits, gnw, m2, lev, hmb, seg)


def _conv_kernel(a_ref, b_ref, dww_ref, dwb_ref, lnw_ref, lnb_ref, pww_ref, pwb_ref, nw_ref,
                 o_ref, pad_ref, dw_ref):
    seq = a_ref.shape[0]
    halo = jnp.zeros((CONV_HALO, D_CONV), F32)
    pad_ref[0:CONV_HALO, :] = halo
    pad_ref[CONV_HALO + seq:CONV_HALO + seq + CONV_HALO, :] = halo
    pad_ref[CONV_HALO:CONV_HALO + seq, :] = a_ref[...] * jax.nn.sigmoid(b_ref[...])

    win_rows = CONV_ROWS + 2 * CONV_HALO

    def tap_body(i, carry):
        r0 = pl.multiple_of(i * CONV_ROWS, CONV_ROWS)
        win = pad_ref[pl.ds(r0, win_rows), :]
        acc = jnp.zeros((CONV_ROWS, D_CONV), F32)
        for r in range(SUBLANES):
            shifted = win if r == 0 else pltpu.roll(win, win_rows - r, 0)
            for a8 in range(0, 2 * CONV_HALO, SUBLANES):
                tap = a8 + r - (CONV_HALO - CONV_PAD)
                if 0 <= tap < CONV_WIDTH:
                    acc = acc + shifted[a8:a8 + CONV_ROWS] * dww_ref[tap:tap + 1, :]
        dw_ref[pl.ds(r0, CONV_ROWS), :] = acc + dwb_ref[...]
        return carry

    lax.fori_loop(0, seq // CONV_ROWS, tap_body, 0, unroll=2)

    def post_body(i, carry):
        r0 = pl.multiple_of(i * CONV_POST_ROWS, CONV_POST_ROWS)
        u = dw_ref[pl.ds(r0, CONV_POST_ROWS), :]
        mu = jnp.mean(u, axis=-1, keepdims=True)
        uc = u - mu
        var = jnp.mean(uc * uc, axis=-1, keepdims=True)
        y = uc * lax.rsqrt(var + LN_EPS) * lnw_ref[...] + lnb_ref[...]
        y = y * jax.nn.sigmoid(y)
        y = jnp.dot(y.astype(BF16), pww_ref[...], preferred_element_type=F32) + pwb_ref[...]
        ms = jnp.mean(y * y, axis=-1, keepdims=True)
        o_ref[pl.ds(r0, CONV_POST_ROWS), :] = (y * lax.rsqrt(ms + EPS) * nw_ref[...]).astype(BF16)
        return carry

    lax.fori_loop(0, seq // CONV_POST_ROWS, post_body, 0, unroll=2)


def _conv(cv, dww, dwb, lnw, lnb, pww, pwb, nw, batch, seq):
    tokens = cv.shape[0]
    col = lambda j: pl.BlockSpec((seq, D_CONV), lambda b: (b, j))
    vec = _full((1, D_CONV))
    return pl.pallas_call(
        _conv_kernel,
        grid=(batch,),
        in_specs=[col(0), col(1), _full(dww.shape), vec, vec, vec, _full((D_CONV, D_CONV)), vec, vec],
        out_specs=pl.BlockSpec((seq, D_CONV), lambda b: (b, 0)),
        out_shape=jax.ShapeDtypeStruct((tokens, D_CONV), BF16),
        scratch_shapes=[pltpu.VMEM((seq + 2 * CONV_HALO, D_CONV), F32), pltpu.VMEM((seq, D_CONV), F32)],
        compiler_params=_params(1),
        name="conv",
    )(cv, cv, dww, dwb, lnw, lnb, pww, pwb, nw)


def _ffn_kernel(x_ref, ya_ref, yh_ref, yc_ref, wo_ref, nw_ref, wg_ref, wu_ref, wd_ref,
                o_ref, act_ref):
    x1 = x_ref[...]
    x1 = x1 + jnp.dot(ya_ref[...], wo_ref[0:D_ATTN, :], preferred_element_type=F32)
    x1 = x1 + jnp.dot(yh_ref[...], wo_ref[D_ATTN:D_ATTN + D_HGRN, :], preferred_element_type=F32)
    x1 = x1 + jnp.dot(yc_ref[...], wo_ref[D_ATTN + D_HGRN:D_MODEL, :], preferred_element_type=F32)
    ms = jnp.mean(x1 * x1, axis=-1, keepdims=True)
    h = (x1 * lax.rsqrt(ms + EPS) * nw_ref[...]).astype(BF16)
    for c in range(0, D_FF, FF_CHUNK):
        g = jnp.dot(h, wg_ref[:, c:c + FF_CHUNK], preferred_element_type=F32)
        u = jnp.dot(h, wu_ref[:, c:c + FF_CHUNK], preferred_element_type=F32)
        act_ref[:, c:c + FF_CHUNK] = (g * jax.nn.sigmoid(g) * u).astype(BF16)
    o_ref[...] = x1 + jnp.dot(act_ref[...], wd_ref[...], preferred_element_type=F32)


def _ffn(x2, ya, yh, yc, wo, nw, wg, wu, wd):
    tokens = x2.shape[0]
    row = lambda w: pl.BlockSpec((TM_FFN, w), lambda i: (i, 0))
    resident = lambda shape: pl.BlockSpec(shape, lambda i: (0, 0), pipeline_mode=pl.Buffered(1))
    return pl.pallas_call(
        _ffn_kernel,
        grid=(tokens // TM_FFN,),
        in_specs=[row(D_MODEL), row(D_ATTN), row(D_HGRN), row(D_CONV),
                  resident((D_MODEL, D_MODEL)), _full((1, D_MODEL)),
                  resident((D_MODEL, D_FF)), resident((D_MODEL, D_FF)), resident((D_FF, D_MODEL))],
        out_specs=row(D_MODEL),
        out_shape=jax.ShapeDtypeStruct((tokens, D_MODEL), F32),
        scratch_shapes=[pltpu.VMEM((TM_FFN, D_FF), BF16)],
        compiler_params=_params(1),
        name="ffn",
    )(x2, ya, yh, yc, wo, nw, wg, wu, wd)


def _rope_tables(seq):
    rows = seq // GRID_W
    row_id = jnp.repeat(jnp.arange(rows, dtype=F32), GRID_W)
    col_id = jnp.tile(jnp.arange(GRID_W, dtype=F32), rows)
    half = HEAD_DIM // 2
    inv_freq = ROPE_THETA ** (-jnp.arange(0, half, 2, dtype=F32) / half)
    ang_r = row_id[:, None] * inv_freq[None, :]
    ang_c = col_id[:, None] * inv_freq[None, :]
    ang = jnp.concatenate([ang_r, ang_r, ang_c, ang_c], axis=-1)
    cos, sin = jnp.cos(ang), jnp.sin(ang)
    lane = np.arange(HEAD_DIM)
    sign = jnp.asarray(np.where((lane % (HEAD_DIM // 2)) < HEAD_DIM // 4, -1.0, 1.0), F32)
    sin_signed = sin * sign[None, :]
    scale = HEAD_DIM ** -0.5 * np.log2(np.e)
    return (jnp.tile(cos, (1, N_HEADS)) * scale, jnp.tile(sin_signed, (1, N_HEADS)) * scale,
            jnp.tile(cos, (1, N_KV_HEADS)), jnp.tile(sin_signed, (1, N_KV_HEADS)))


def _segment_matrix(width, seg_len):
    seg = np.arange(width) // seg_len
    return jnp.asarray((seg[:, None] == seg[None, :]).astype(np.float32), BF16)


def kernel(x, mix_norm_w, w_in, q_norm_w, k_norm_w, hgrn_lb_logits, hgrn_gnorm_w, conv_dw_w, conv_dw_b,
           conv_ln_w, conv_ln_b, conv_pw_w, conv_pw_b, attn_out_norm_w, conv_out_norm_w, w_out,
           ffn_norm_w, w_gate, w_up, w_down):
    batch, seq, d_model = x.shape
    assert d_model == D_MODEL and seq % TM_PROJ == 0 and seq % HG_CHUNK == 0
    assert seq % (ATTN_SUBTILES * TQ_ATTN) == 0 and seq % CONV_POST_ROWS == 0 and seq % HG_FINISH_ROWS == 0
    assert (batch * seq) % TM_FFN == 0 and w_in.shape == (DEPTH, D_MODEL, D_IN_PROJ)
    assert w_gate.shape == (DEPTH, D_MODEL, D_FF)

    tabs = _rope_tables(seq)
    segq = _segment_matrix(D_ATTN, HEAD_DIM)
    segk = _segment_matrix(D_KV, HEAD_DIM)
    segh = _segment_matrix(D_HGRN, HEAD_DIM)
    m2, lev, hm = _hgrn_constants()
    hconsts = (jnp.asarray(m2, BF16), jnp.asarray(lev), jnp.asarray(hm, BF16))
    logits = hgrn_lb_logits.astype(F32)

    x2 = x.reshape(batch * seq, D_MODEL)
    for l in range(DEPTH):
        q, kp, va, hg, cv = _inproj(
            x2, mix_norm_w[l][None, :], w_in[l].astype(BF16),
            jnp.tile(q_norm_w[l], N_HEADS)[None, :], jnp.tile(k_norm_w[l], N_KV_HEADS)[None, :],
            tabs, segq, segk, batch, seq)
        ya = _attention(q, kp, va, attn_out_norm_w[l][None, :], batch, seq)
        yh = _hgrn(hg, logits, jnp.tile(hgrn_gnorm_w[l], HGRN_HEADS)[None, :], hconsts, segh, l, batch, seq)
        yc = _conv(cv, conv_dw_w[l], conv_dw_b[l][None, :], conv_ln_w[l][None, :], conv_ln_b[l][None, :],
                   conv_pw_w[l].astype(BF16), conv_pw_b[l][None, :], conv_out_norm_w[l][None, :], batch, seq)
        x2 = _ffn(x2, ya, yh, yc, w_out[l].astype(BF16), ffn_norm_w[l][None, :],
                  w_gate[l].astype(BF16), w_up[l].astype(BF16), w_down[l].astype(BF16))
    return x2.reshape(batch, seq, D_MODEL)
```

```python
import functools

import numpy as np
import jax
import jax.numpy as jnp
from jax import lax
from jax.experimental import pallas as pl
from jax.experimental.pallas import tpu as pltpu

F32 = jnp.float32
BF16 = jnp.bfloat16

D_MODEL = 1024
DEPTH = 2
GRID_W = 64
D_ATTN = 512
D_HGRN = 256
D_CONV = 256
HEAD_DIM = 64
N_HEADS = D_ATTN // HEAD_DIM
N_KV_HEADS = 2
D_KV = N_KV_HEADS * HEAD_DIM
ROPE_THETA = 10000.0
HGRN_HEADS = D_HGRN // HEAD_DIM
F_MIN = 1e-6
LOG2_E = float(np.log2(np.e))
CONV_WIDTH = 31
CONV_PAD = (CONV_WIDTH - 1) // 2
D_FF = 2816
EPS = 1e-6
LN_EPS = 1e-5
D_IN_PROJ = D_ATTN + 2 * D_KV + 5 * D_HGRN + 2 * D_CONV

OFF_Q = 0
OFF_K = OFF_Q + D_ATTN
OFF_V = OFF_K + D_KV
OFF_H = OFF_V + D_KV
OFF_C = OFF_H + 5 * D_HGRN

LANES = 128
SUBLANES = 8
VMEM_LIMIT_BYTES = 56 * 1024 * 1024

TM_PROJ = 1024
TQ_ATTN = 256
ATTN_SUBTILES = 4
ATTN_AHEAD = 3
ATTN_SLOTS = ATTN_AHEAD + 1
TM_FFN = 1024
FF_CHUNK = 256
HG_CHUNK = 64
HG_FINE = 8
HG_REF = 2
HG_FINISH_ROWS = 256
HG_UNROLL = 4
CONV_ROWS = 128
CONV_POST_ROWS = 256
CONV_HALO = 16

assert -np.log(F_MIN) * (HG_FINE - 1 - HG_REF) < 80.0 and -np.log(F_MIN) * HG_REF < 80.0


def _params(n_grid_axes):
    return pltpu.CompilerParams(
        dimension_semantics=("arbitrary",) * n_grid_axes,
        vmem_limit_bytes=VMEM_LIMIT_BYTES)


def _full(shape):
    nd = len(shape)
    return pl.BlockSpec(shape, lambda *_: (0,) * nd)


def _segment_mean_sq(y, seg_ref, seg_len):
    return jnp.dot((y * y).astype(BF16), seg_ref[...], preferred_element_type=F32) * (1.0 / seg_len)


def _rope(xn, cos, sin_signed, first_mask):
    width = xn.shape[1]
    left = pltpu.roll(xn, width - HEAD_DIM // 4, 1)
    right = pltpu.roll(xn, HEAD_DIM // 4, 1)
    rot = jnp.where(first_mask, left, right)
    return xn * cos + rot * sin_signed


def _inproj_kernel(x_ref, nw_ref, w_ref, qnw_ref, knw_ref, cq_ref, sq_ref, ck_ref, sk_ref,
                   segq_ref, segk_ref,
                   q_out, kp_out, va_out, hg_out, cv_out):
    x = x_ref[...]
    ms = jnp.mean(x * x, axis=-1, keepdims=True)
    h = (x * lax.rsqrt(ms + EPS) * nw_ref[...]).astype(BF16)

    q = jnp.dot(h, w_ref[:, OFF_Q:OFF_K], preferred_element_type=F32)
    qn = q * lax.rsqrt(_segment_mean_sq(q, segq_ref, HEAD_DIM) + EPS) * qnw_ref[...]
    lane_q = lax.broadcasted_iota(jnp.int32, (1, D_ATTN), 1)
    first_q = (lane_q & (HEAD_DIM // 2 - 1)) < HEAD_DIM // 4
    q_out[...] = _rope(qn, cq_ref[...], sq_ref[...], first_q).T.astype(BF16)

    k = jnp.dot(h, w_ref[:, OFF_K:OFF_V], preferred_element_type=F32)
    kn = k * lax.rsqrt(_segment_mean_sq(k, segk_ref, HEAD_DIM) + EPS) * knw_ref[...]
    lane_k = lax.broadcasted_iota(jnp.int32, (1, D_KV), 1)
    first_k = (lane_k & (HEAD_DIM // 2 - 1)) < HEAD_DIM // 4
    kr = _rope(kn, ck_ref[...], sk_ref[...], first_k)
    ks = pltpu.roll(kr, HEAD_DIM, 1)
    low = lane_k < HEAD_DIM
    zero = jnp.zeros_like(kr)
    kp_out[0] = jnp.where(low, kr, zero).astype(BF16)
    kp_out[1] = jnp.where(low, zero, ks).astype(BF16)
    kp_out[2] = jnp.where(low, ks, zero).astype(BF16)
    kp_out[3] = jnp.where(low, zero, kr).astype(BF16)

    v = jnp.dot(h, w_ref[:, OFF_V:OFF_H], preferred_element_type=F32)
    va_out[...] = v.T.astype(BF16)

    hg_out[...] = jnp.dot(h, w_ref[:, OFF_H:OFF_C], preferred_element_type=F32)
    cv_out[...] = jnp.dot(h, w_ref[:, OFF_C:D_IN_PROJ], preferred_element_type=F32)


def _inproj(x2, nw, w_in, qnw, knw, tabs, segq, segk, batch, seq):
    tokens = x2.shape[0]
    n_seq_tiles = seq // TM_PROJ
    cq, sq, ck, sk = tabs
    tab_q = pl.BlockSpec((TM_PROJ, D_ATTN), lambda i: (i % n_seq_tiles, 0))
    tab_k = pl.BlockSpec((TM_PROJ, D_KV), lambda i: (i % n_seq_tiles, 0))
    row = lambda w: pl.BlockSpec((TM_PROJ, w), lambda i: (i, 0))
    return pl.pallas_call(
        _inproj_kernel,
        grid=(tokens // TM_PROJ,),
        in_specs=[row(D_MODEL), _full((1, D_MODEL)), _full((D_MODEL, D_IN_PROJ)),
                  _full((1, D_ATTN)), _full((1, D_KV)), tab_q, tab_q, tab_k, tab_k,
                  _full((D_ATTN, D_ATTN)), _full((D_KV, D_KV))],
        out_specs=[pl.BlockSpec((None, D_ATTN, TM_PROJ), lambda i: (i // n_seq_tiles, 0, i % n_seq_tiles)),
                   pl.BlockSpec((2 * N_KV_HEADS, TM_PROJ, D_KV), lambda i: (0, i, 0)),
                   pl.BlockSpec((None, D_KV, TM_PROJ), lambda i: (i // n_seq_tiles, 0, i % n_seq_tiles)),
                   row(5 * D_HGRN), row(2 * D_CONV)],
        out_shape=[jax.ShapeDtypeStruct((batch, D_ATTN, seq), BF16),
                   jax.ShapeDtypeStruct((2 * N_KV_HEADS, tokens, D_KV), BF16),
                   jax.ShapeDtypeStruct((batch, D_KV, seq), BF16),
                   jax.ShapeDtypeStruct((tokens, 5 * D_HGRN), F32),
                   jax.ShapeDtypeStruct((tokens, 2 * D_CONV), F32)],
        compiler_params=_params(1),
        name="inproj",
    )(x2, nw, w_in, qnw, knw, cq, sq, ck, sk, segq, segk)


def _attn_kernel(q_ref, kp_ref, va_ref, nw_ref, o_ref, st_ref, yt_ref):
    n_items = ATTN_SUBTILES * N_HEADS

    def scores_t(it):
        sub, hd = divmod(it, N_HEADS)
        g = hd // (N_HEADS // N_KV_HEADS)
        qp = q_ref[(hd // 2) * LANES:(hd // 2 + 1) * LANES, sub * TQ_ATTN:(sub + 1) * TQ_ATTN]
        st_ref[it % ATTN_SLOTS] = jnp.dot(kp_ref[2 * g + hd % 2], qp, preferred_element_type=F32)

    for it in range(ATTN_AHEAD):
        scores_t(it)
    for it in range(n_items):
        sub, hd = divmod(it, N_HEADS)
        g = hd // (N_HEADS // N_KV_HEADS)
        if it + ATTN_AHEAD < n_items:
            scores_t(it + ATTN_AHEAD)
        slot = it % ATTN_SLOTS
        m = jnp.max(st_ref[slot], axis=0, keepdims=True)
        p = jnp.exp2(st_ref[slot] - m)
        den = jnp.sum(p, axis=0, keepdims=True)
        num = jnp.dot(va_ref[g * HEAD_DIM:(g + 1) * HEAD_DIM, :], p.astype(BF16),
                      preferred_element_type=F32)
        yt_ref[hd * HEAD_DIM:(hd + 1) * HEAD_DIM, :] = num / den
        if hd == N_HEADS - 1:
            y = yt_ref[...].T
            ms = jnp.mean(y * y, axis=-1, keepdims=True)
            o_ref[sub * TQ_ATTN:(sub + 1) * TQ_ATTN, :] = (y * lax.rsqrt(ms + EPS) * nw_ref[...]).astype(BF16)


def _attention(q, kp, va, nw, batch, seq):
    tokens = batch * seq
    rows = ATTN_SUBTILES * TQ_ATTN
    nq = seq // rows
    return pl.pallas_call(
        _attn_kernel,
        grid=(batch, nq),
        in_specs=[pl.BlockSpec((None, D_ATTN, rows), lambda b, j: (b, 0, j)),
                  pl.BlockSpec((2 * N_KV_HEADS, seq, D_KV), lambda b, j: (0, b, 0)),
                  pl.BlockSpec((None, D_KV, seq), lambda b, j: (b, 0, 0)),
                  pl.BlockSpec((1, D_ATTN), lambda b, j: (0, 0))],
        out_specs=pl.BlockSpec((rows, D_ATTN), lambda b, j: (b * nq + j, 0)),
        out_shape=jax.ShapeDtypeStruct((tokens, D_ATTN), BF16),
        scratch_shapes=[pltpu.VMEM((ATTN_SLOTS, seq, TQ_ATTN), F32), pltpu.VMEM((D_ATTN, TQ_ATTN), F32)],
        compiler_params=_params(2),
        name="attn",
    )(q, kp, va, nw)


HG_LEVELS = (32, 16, 8)


def _hgrn_constants():
    n = HG_CHUNK
    idx = np.arange(n)
    fwd = (idx[None, :] <= idx[:, None]).astype(np.float32)
    bwd = fwd[::-1, ::-1]
    m2 = np.stack([np.tile(fwd, (1, 2)), np.tile(bwd, (1, 2))])

    lev = np.full((n, n), len(HG_LEVELS) + 1, np.int32)
    for t in range(n):
        for s in range(t + 1):
            if t // HG_FINE == s // HG_FINE:
                lev[t, s] = len(HG_LEVELS)
            else:
                for li, half in enumerate(HG_LEVELS):
                    if t // (2 * half) == s // (2 * half) and t // half != s // half:
                        lev[t, s] = li
                        break
    lev2 = np.stack([np.tile(lev, (1, HGRN_HEADS)), np.tile(lev[::-1, ::-1], (1, HGRN_HEADS))])
    head = np.arange(D_HGRN) // HEAD_DIM
    headmask = (head[:, None] == head[None, :]).astype(np.float32)
    return m2, lev2, headmask


def _hgrn_kernel(hq_ref, zf_ref, zb_ref, hi_ref, gate_ref, logit_ref, gnw_ref, m2_ref, lev_ref,
                 hmb_ref, seg_ref, o_ref, ofw_ref, obw_ref, st_ref, *, layer):
    seq = hq_ref.shape[0]
    n_chunks = seq // HG_CHUNK
    n = HG_CHUNK

    lg = [logit_ref[i] for i in range(DEPTH)]
    mx = functools.reduce(jnp.maximum, lg)
    ex = [jnp.exp(v - mx) for v in lg]
    den = functools.reduce(lambda a, b: a + b, ex)
    sm = [e / den for e in ex]
    acc = sm[0]
    for i in range(1, layer + 1):
        acc = acc + sm[i]
    lb = acc - sm[0]

    def stage_a(d, z_ref, lb_row, r0):
        z = z_ref[pl.ds(r0, n), :]
        f = lb_row + (1.0 - lb_row) * jax.nn.sigmoid(z)
        g = jnp.log(jnp.maximum(f, F_MIN)) * LOG2_E
        kk = 1.0 - f
        g1 = g.astype(BF16)
        g2 = (g - g1.astype(F32)).astype(BF16)
        gcat = jnp.concatenate([g1, g2], axis=0)
        e_cum = jnp.dot(m2_ref[d], gcat, preferred_element_type=F32)
        return kk, e_cum

    def level_exponent(e_cum, d, li):
        fine = li == len(HG_LEVELS)
        blk = HG_FINE if fine else HG_LEVELS[li]
        parts = []
        for r in range(0, n, blk):
            p_start = r if d == 0 else n - blk - r
            if fine:
                p_ref, later = p_start + HG_REF, True
            else:
                p_ref = (p_start // (2 * blk)) * 2 * blk + blk
                later = p_start >= p_ref
            t_ref = p_ref if d == 0 else n - 1 - p_ref
            ref = e_cum[t_ref:t_ref + 1]
            rows = e_cum[r:r + blk]
            parts.append(rows - ref if later else ref - rows)
        return jnp.concatenate(parts, axis=0)

    def stage_b(d, r0, kk, e_cum):
        hm_bf = hmb_ref[...]
        q = hq_ref[pl.ds(r0, n), :]
        v = hi_ref[pl.ds(r0, n), :]
        lev = lev_ref[d]
        a = jnp.zeros((n, D_HGRN), F32)
        for li in range(len(HG_LEVELS) + 1):
            e_lev = level_exponent(e_cum, d, li)
            x_q = jnp.exp2(e_lev)
            x_k = x_q if li < len(HG_LEVELS) else jnp.exp2(-e_lev)
            qt = (q * x_q).astype(BF16)
            kt = (kk * x_k).astype(BF16)
            kbd = jnp.concatenate([kt] * HGRN_HEADS, axis=0) * hm_bf
            sc = lax.dot_general(qt, kbd, (((1,), (1,)), ((), ())), preferred_element_type=F32)
            a = jnp.where(lev == li, sc, a)
        return a.astype(BF16)

    def stage_b2(d, r0, kk, e_cum, a):
        hm_bf = hmb_ref[...]
        q = hq_ref[pl.ds(r0, n), :]
        v = hi_ref[pl.ds(r0, n), :]
        vbd = jnp.concatenate([v.astype(BF16)] * HGRN_HEADS, axis=0) * hm_bf
        o_intra = jnp.dot(a, vbd, preferred_element_type=F32)

        e_tot = e_cum[n - 1:n] if d == 0 else e_cum[0:1]
        qb = (q * jnp.exp2(e_cum)).astype(BF16)
        kdec = kk * jnp.exp2(e_tot - e_cum)
        zpad = jnp.zeros((LANES - n - SUBLANES, D_HGRN), F32)
        tot8 = jnp.broadcast_to(e_tot, (SUBLANES, D_HGRN))
        kt = jnp.concatenate([kdec, tot8, zpad], axis=0).T
        decay_col = jnp.exp2(kt[:, n:n + 1])
        vpad = jnp.concatenate([v.astype(BF16), jnp.zeros((LANES - n, D_HGRN), BF16)], axis=0)
        upd = jnp.dot(kt.astype(BF16), vpad, preferred_element_type=F32)
        return o_intra, qb, upd, decay_col

    def stage_c(d, o_intra, qb, upd, decay_col):
        st = st_ref[d]
        o = o_intra + jnp.dot(qb, st.astype(BF16) * hmb_ref[...], preferred_element_type=F32)
        st_ref[d] = st * decay_col + upd
        return o

    st_ref[...] = jnp.zeros_like(st_ref)

    def scan_body(ci, carry):
        items = []
        for u in range(HG_UNROLL):
            cf = ci * HG_UNROLL + u
            items.append((0, zf_ref, lb[0:1], pl.multiple_of(cf * n, n), ofw_ref))
            items.append((1, zb_ref, lb[1:2], pl.multiple_of((n_chunks - 1 - cf) * n, n), obw_ref))
        sa = [stage_a(d, z_ref, lb_row, r0) for d, z_ref, lb_row, r0, _ in items]
        sb = [stage_b(d, r0, *a) for (d, _, _, r0, _), a in zip(items, sa)]
        sb2 = [stage_b2(d, r0, *a, b) for (d, _, _, r0, _), a, b in zip(items, sa, sb)]
        for (d, _, _, r0, out_ref), b in zip(items, sb2):
            out_ref[pl.ds(r0, n), :] = stage_c(d, *b)
        return carry

    lax.fori_loop(0, n_chunks // HG_UNROLL, scan_body, 0)

    gnw = gnw_ref[...]

    def finish_body(i, carry):
        r0 = pl.multiple_of(i * HG_FINISH_ROWS, HG_FINISH_ROWS)
        o = ofw_ref[pl.ds(r0, HG_FINISH_ROWS), :] + obw_ref[pl.ds(r0, HG_FINISH_ROWS), :]
        on = o * lax.rsqrt(_segment_mean_sq(o, seg_ref, HEAD_DIM) + EPS) * gnw
        gate = gate_ref[pl.ds(r0, HG_FINISH_ROWS), :]
        o_ref[pl.ds(r0, HG_FINISH_ROWS), :] = (on * (gate * jax.nn.sigmoid(gate))).astype(BF16)
        return carry

    lax.fori_loop(0, seq // HG_FINISH_ROWS, finish_body, 0, unroll=2)


def _hgrn(hg, logits, gnw, consts, seg, layer, batch, seq):
    tokens = hg.shape[0]
    m2, lev, hmb = consts
    col = lambda j: pl.BlockSpec((seq, D_HGRN), lambda b: (b, j))
    return pl.pallas_call(
        functools.partial(_hgrn_kernel, layer=layer),
        grid=(batch,),
        in_specs=[col(0), col(1), col(2), col(3), col(4),
                  _full(logits.shape), _full((1, D_HGRN)), _full(m2.shape), _full(lev.shape),
                  _full(hmb.shape), _full(seg.shape)],
        out_specs=pl.BlockSpec((seq, D_HGRN), lambda b: (b, 0)),
        out_shape=jax.ShapeDtypeStruct((tokens, D_HGRN), BF16),
        scratch_shapes=[pltpu.VMEM((seq, D_HGRN), F32), pltpu.VMEM((seq, D_HGRN), F32),
                        pltpu.VMEM((2, D_HGRN, D_HGRN), F32)],
        compiler_params=_params(1),
        name="hgrn",
    )(hg, hg, hg, hg, hg, logits, gnw, m2, lev, hmb, seg)


def _conv_kernel(a_ref, b_ref, dww_ref, dwb_ref, lnw_ref, lnb_ref, pww_ref, pwb_ref, nw_ref,
                 o_ref, pad_ref, dw_ref):
    seq = a_ref.shape[0]
    halo = jnp.zeros((CONV_HALO, D_CONV), F32)
    pad_ref[0:CONV_HALO, :] = halo
    pad_ref[CONV_HALO + seq:CONV_HALO + seq + CONV_HALO, :] = halo
    pad_ref[CONV_HALO:CONV_HALO + seq, :] = a_ref[...] * jax.nn.sigmoid(b_ref[...])

    win_rows = CONV_ROWS + 2 * CONV_HALO

    def tap_body(i, carry):
        r0 = pl.multiple_of(i * CONV_ROWS, CONV_ROWS)
        win = pad_ref[pl.ds(r0, win_rows), :]
        acc = jnp.zeros((CONV_ROWS, D_CONV), F32)
        for r in range(SUBLANES):
            shifted = win if r == 0 else pltpu.roll(win, win_rows - r, 0)
            for a8 in range(0, 2 * CONV_HALO, SUBLANES):
                tap = a8 + r - (CONV_HALO - CONV_PAD)
                if 0 <= tap < CONV_WIDTH:
                    acc = acc + shifted[a8:a8 + CONV_ROWS] * dww_ref[tap:tap + 1, :]
        dw_ref[pl.ds(r0, CONV_ROWS), :] = acc + dwb_ref[...]
        return carry

    lax.fori_loop(0, seq // CONV_ROWS, tap_body, 0, unroll=2)

    def post_body(i, carry):
        r0 = pl.multiple_of(i * CONV_POST_ROWS, CONV_POST_ROWS)
        u = dw_ref[pl.ds(r0, CONV_POST_ROWS), :]
        mu = jnp.mean(u, axis=-1, keepdims=True)
        uc = u - mu
        var = jnp.mean(uc * uc, axis=-1, keepdims=True)
        y = uc * lax.rsqrt(var + LN_EPS) * lnw_ref[...] + lnb_ref[...]
        y = y * jax.nn.sigmoid(y)
        y = jnp.dot(y.astype(BF16), pww_ref[...], preferred_element_type=F32) + pwb_ref[...]
        ms = jnp.mean(y * y, axis=-1, keepdims=True)
        o_ref[pl.ds(r0, CONV_POST_ROWS), :] = (y * lax.rsqrt(ms + EPS) * nw_ref[...]).astype(BF16)
        return carry

    lax.fori_loop(0, seq // CONV_POST_ROWS, post_body, 0, unroll=2)


def _conv(cv, dww, dwb, lnw, lnb, pww, pwb, nw, batch, seq):
    tokens = cv.shape[0]
    col = lambda j: pl.BlockSpec((seq, D_CONV), lambda b: (b, j))
    vec = _full((1, D_CONV))
    return pl.pallas_call(
        _conv_kernel,
        grid=(batch,),
        in_specs=[col(0), col(1), _full(dww.shape), vec, vec, vec, _full((D_CONV, D_CONV)), vec, vec],
        out_specs=pl.BlockSpec((seq, D_CONV), lambda b: (b, 0)),
        out_shape=jax.ShapeDtypeStruct((tokens, D_CONV), BF16),
        scratch_shapes=[pltpu.VMEM((seq + 2 * CONV_HALO, D_CONV), F32), pltpu.VMEM((seq, D_CONV), F32)],
        compiler_params=_params(1),
        name="conv",
    )(cv, cv, dww, dwb, lnw, lnb, pww, pwb, nw)


def _ffn_kernel(x_ref, ya_ref, yh_ref, yc_ref, wo_ref, nw_ref, wg_ref, wu_ref, wd_ref,
                o_ref, act_ref):
    x1 = x_ref[...]
    x1 = x1 + jnp.dot(ya_ref[...], wo_ref[0:D_ATTN, :], preferred_element_type=F32)
    x1 = x1 + jnp.dot(yh_ref[...], wo_ref[D_ATTN:D_ATTN + D_HGRN, :], preferred_element_type=F32)
    x1 = x1 + jnp.dot(yc_ref[...], wo_ref[D_ATTN + D_HGRN:D_MODEL, :], preferred_element_type=F32)
    ms = jnp.mean(x1 * x1, axis=-1, keepdims=True)
    h = (x1 * lax.rsqrt(ms + EPS) * nw_ref[...]).astype(BF16)
    for c in range(0, D_FF, FF_CHUNK):
        g = jnp.dot(h, wg_ref[:, c:c + FF_CHUNK], preferred_element_type=F32)
        u = jnp.dot(h, wu_ref[:, c:c + FF_CHUNK], preferred_element_type=F32)
        act_ref[:, c:c + FF_CHUNK] = (g * jax.nn.sigmoid(g) * u).astype(BF16)
    o_ref[...] = x1 + jnp.dot(act_ref[...], wd_ref[...], preferred_element_type=F32)


def _ffn(x2, ya, yh, yc, wo, nw, wg, wu, wd):
    tokens = x2.shape[0]
    row = lambda w: pl.BlockSpec((TM_FFN, w), lambda i: (i, 0))
    resident = lambda shape: pl.BlockSpec(shape, lambda i: (0, 0), pipeline_mode=pl.Buffered(1))
    return pl.pallas_call(
        _ffn_kernel,
        grid=(tokens // TM_FFN,),
        in_specs=[row(D_MODEL), row(D_ATTN), row(D_HGRN), row(D_CONV),
                  resident((D_MODEL, D_MODEL)), _full((1, D_MODEL)),
                  resident((D_MODEL, D_FF)), resident((D_MODEL, D_FF)), resident((D_FF, D_MODEL))],
        out_specs=row(D_MODEL),
        out_shape=jax.ShapeDtypeStruct((tokens, D_MODEL), F32),
        scratch_shapes=[pltpu.VMEM((TM_FFN, D_FF), BF16)],
        compiler_params=_params(1),
        name="ffn",
    )(x2, ya, yh, yc, wo, nw, wg, wu, wd)


def _rope_tables(seq):
    rows = seq // GRID_W
    row_id = jnp.repeat(jnp.arange(rows, dtype=F32), GRID_W)
    col_id = jnp.tile(jnp.arange(GRID_W, dtype=F32), rows)
    half = HEAD_DIM // 2
    inv_freq = ROPE_THETA ** (-jnp.arange(0, half, 2, dtype=F32) / half)
    ang_r = row_id[:, None] * inv_freq[None, :]
    ang_c = col_id[:, None] * inv_freq[None, :]
    ang = jnp.concatenate([ang_r, ang_r, ang_c, ang_c], axis=-1)
    cos, sin = jnp.cos(ang), jnp.sin(ang)
    lane = np.arange(HEAD_DIM)
    sign = jnp.asarray(np.where((lane % (HEAD_DIM // 2)) < HEAD_DIM // 4, -1.0, 1.0), F32)
    sin_signed = sin * sign[None, :]
    scale = HEAD_DIM ** -0.5 * np.log2(np.e)
    return (jnp.tile(cos, (1, N_HEADS)) * scale, jnp.tile(sin_signed, (1, N_HEADS)) * scale,
            jnp.tile(cos, (1, N_KV_HEADS)), jnp.tile(sin_signed, (1, N_KV_HEADS)))


def _segment_matrix(width, seg_len):
    seg = np.arange(width) // seg_len
    return jnp.asarray((seg[:, None] == seg[None, :]).astype(np.float32), BF16)


def kernel(x, mix_norm_w, w_in, q_norm_w, k_norm_w, hgrn_lb_logits, hgrn_gnorm_w, conv_dw_w, conv_dw_b,
           conv_ln_w, conv_ln_b, conv_pw_w, conv_pw_b, attn_out_norm_w, conv_out_norm_w, w_out,
           ffn_norm_w, w_gate, w_up, w_down):
    batch, seq, d_model = x.shape
    assert d_model == D_MODEL and seq % TM_PROJ == 0 and seq % HG_CHUNK == 0
    assert seq % (ATTN_SUBTILES * TQ_ATTN) == 0 and seq % CONV_POST_ROWS == 0 and seq % HG_FINISH_ROWS == 0
    assert (batch * seq) % TM_FFN == 0 and w_in.shape == (DEPTH, D_MODEL, D_IN_PROJ)
    assert w_gate.shape == (DEPTH, D_MODEL, D_FF)

    tabs = _rope_tables(seq)
    segq = _segment_matrix(D_ATTN, HEAD_DIM)
    segk = _segment_matrix(D_KV, HEAD_DIM)
    segh = _segment_matrix(D_HGRN, HEAD_DIM)
    m2, lev, hm = _hgrn_constants()
    hconsts = (jnp.asarray(m2, BF16), jnp.asarray(lev), jnp.asarray(hm, BF16))
    logits = hgrn_lb_logits.astype(F32)

    x2 = x.reshape(batch * seq, D_MODEL)
    for l in range(DEPTH):
        q, kp, va, hg, cv = _inproj(
            x2, mix_norm_w[l][None, :], w_in[l].astype(BF16),
            jnp.tile(q_norm_w[l], N_HEADS)[None, :], jnp.tile(k_norm_w[l], N_KV_HEADS)[None, :],
            tabs, segq, segk, batch, seq)
        ya = _attention(q, kp, va, attn_out_norm_w[l][None, :], batch, seq)
        yh = _hgrn(hg, logits, jnp.tile(hgrn_gnorm_w[l], HGRN_HEADS)[None, :], hconsts, segh, l, batch, seq)
        yc = _conv(cv, conv_dw_w[l], conv_dw_b[l][None, :], conv_ln_w[l][None, :], conv_ln_b[l][None, :],
                   conv_pw_w[l].astype(BF16), conv_pw_b[l][None, :], conv_out_norm_w[l][None, :], batch, seq)
        x2 = _ffn(x2, ya, yh, yc, w_out[l].astype(BF16), ffn_norm_w[l][None, :],
                  w_gate[l].astype(BF16), w_up[l].astype(BF16), w_down[l].astype(BF16))
    return x2.reshape(batch, seq, D_MODEL)
```

```python
import functools

import numpy as np
import jax
import jax.numpy as jnp
from jax import lax
from jax.experimental import pallas as pl
from jax.experimental.pallas import tpu as pltpu

F32 = jnp.float32
BF16 = jnp.bfloat16

D_MODEL = 1024
DEPTH = 2
GRID_W = 64
D_ATTN = 512
D_HGRN = 256
D_CONV = 256
HEAD_DIM = 64
N_HEADS = D_ATTN // HEAD_DIM
N_KV_HEADS = 2
D_KV = N_KV_HEADS * HEAD_DIM
ROPE_THETA = 10000.0
HGRN_HEADS = D_HGRN // HEAD_DIM
F_MIN = 1e-6
LOG2_E = float(np.log2(np.e))
CONV_WIDTH = 31
CONV_PAD = (CONV_WIDTH - 1) // 2
D_FF = 2816
EPS = 1e-6
LN_EPS = 1e-5
D_IN_PROJ = D_ATTN + 2 * D_KV + 5 * D_HGRN + 2 * D_CONV

OFF_Q = 0
OFF_K = OFF_Q + D_ATTN
OFF_V = OFF_K + D_KV
OFF_H = OFF_V + D_KV
OFF_C = OFF_H + 5 * D_HGRN

LANES = 128
SUBLANES = 8
VMEM_LIMIT_BYTES = 56 * 1024 * 1024

TM_PROJ = 1024
TQ_ATTN = 256
ATTN_SUBTILES = 4
ATTN_AHEAD = 3
ATTN_SLOTS = ATTN_AHEAD + 1
TM_FFN = 1024
FF_CHUNK = 256
HG_CHUNK = 64
HG_FINE = 8
HG_REF = 2
HG_FINISH_ROWS = 256
HG_UNROLL = 4
CONV_ROWS = 128
CONV_POST_ROWS = 256
CONV_HALO = 16

assert -np.log(F_MIN) * (HG_FINE - 1 - HG_REF) < 80.0 and -np.log(F_MIN) * HG_REF < 80.0


def _params(n_grid_axes):
    return pltpu.CompilerParams(
        dimension_semantics=("arbitrary",) * n_grid_axes,
        vmem_limit_bytes=VMEM_LIMIT_BYTES)


def _full(shape):
    nd = len(shape)
    return pl.BlockSpec(shape, lambda *_: (0,) * nd)


def _segment_mean_sq(y, seg_ref, seg_len):
    return jnp.dot((y * y).astype(BF16), seg_ref[...], preferred_element_type=F32) * (1.0 / seg_len)


def _rope(xn, cos, sin_signed, first_mask):
    width = xn.shape[1]
    left = pltpu.roll(xn, width - HEAD_DIM // 4, 1)
    right = pltpu.roll(xn, HEAD_DIM // 4, 1)
    rot = jnp.where(first_mask, left, right)
    return xn * cos + rot * sin_signed


def _inproj_kernel(x_ref, nw_ref, w_ref, qnw_ref, knw_ref, cq_ref, sq_ref, ck_ref, sk_ref,
                   segq_ref, segk_ref,
                   q_out, kp_out, va_out, hg_out, cv_out):
    x = x_ref[...]
    ms = jnp.mean(x * x, axis=-1, keepdims=True)
    h = (x * lax.rsqrt(ms + EPS) * nw_ref[...]).astype(BF16)

    q = jnp.dot(h, w_ref[:, OFF_Q:OFF_K], preferred_element_type=F32)
    qn = q * lax.rsqrt(_segment_mean_sq(q, segq_ref, HEAD_DIM) + EPS) * qnw_ref[...]
    lane_q = lax.broadcasted_iota(jnp.int32, (1, D_ATTN), 1)
    first_q = (lane_q & (HEAD_DIM // 2 - 1)) < HEAD_DIM // 4
    q_out[...] = _rope(qn, cq_ref[...], sq_ref[...], first_q).T.astype(BF16)

    k = jnp.dot(h, w_ref[:, OFF_K:OFF_V], preferred_element_type=F32)
    kn = k * lax.rsqrt(_segment_mean_sq(k, segk_ref, HEAD_DIM) + EPS) * knw_ref[...]
    lane_k = lax.broadcasted_iota(jnp.int32, (1, D_KV), 1)
    first_k = (lane_k & (HEAD_DIM // 2 - 1)) < HEAD_DIM // 4
    kr = _rope(kn, ck_ref[...], sk_ref[...], first_k)
    ks = pltpu.roll(kr, HEAD_DIM, 1)
    low = lane_k < HEAD_DIM
    zero = jnp.zeros_like(kr)
    kp_out[0] = jnp.where(low, kr, zero).astype(BF16)
    kp_out[1] = jnp.where(low, zero, ks).astype(BF16)
    kp_out[2] = jnp.where(low, ks, zero).astype(BF16)
    kp_out[3] = jnp.where(low, zero, kr).astype(BF16)

    v = jnp.dot(h, w_ref[:, OFF_V:OFF_H], preferred_element_type=F32)
    va_out[...] = v.T.astype(BF16)

    hg_out[...] = jnp.dot(h, w_ref[:, OFF_H:OFF_C], preferred_element_type=F32)
    cv_out[...] = jnp.dot(h, w_ref[:, OFF_C:D_IN_PROJ], preferred_element_type=F32)


def _layer_block(shape, layer, **kwargs):
    return pl.BlockSpec((None,) + tuple(shape), lambda *_: (layer,) + (0,) * len(shape), **kwargs)


def _inproj(x2, nw, w_in, qnw, knw, tabs, segq, segk, layer, batch, seq):
    tokens = x2.shape[0]
    n_seq_tiles = seq // TM_PROJ
    cq, sq, ck, sk = tabs
    tab_q = pl.BlockSpec((TM_PROJ, D_ATTN), lambda i: (i % n_seq_tiles, 0))
    tab_k = pl.BlockSpec((TM_PROJ, D_KV), lambda i: (i % n_seq_tiles, 0))
    row = lambda w: pl.BlockSpec((TM_PROJ, w), lambda i: (i, 0))
    return pl.pallas_call(
        _inproj_kernel,
        grid=(tokens // TM_PROJ,),
        in_specs=[row(D_MODEL), _full((1, D_MODEL)), _layer_block((D_MODEL, D_IN_PROJ), layer),
                  _full((1, D_ATTN)), _full((1, D_KV)), tab_q, tab_q, tab_k, tab_k,
                  _full((D_ATTN, D_ATTN)), _full((D_KV, D_KV))],
        out_specs=[pl.BlockSpec((None, D_ATTN, TM_PROJ), lambda i: (i // n_seq_tiles, 0, i % n_seq_tiles)),
                   pl.BlockSpec((2 * N_KV_HEADS, TM_PROJ, D_KV), lambda i: (0, i, 0)),
                   pl.BlockSpec((None, D_KV, TM_PROJ), lambda i: (i // n_seq_tiles, 0, i % n_seq_tiles)),
                   row(5 * D_HGRN), row(2 * D_CONV)],
        out_shape=[jax.ShapeDtypeStruct((batch, D_ATTN, seq), BF16),
                   jax.ShapeDtypeStruct((2 * N_KV_HEADS, tokens, D_KV), BF16),
                   jax.ShapeDtypeStruct((batch, D_KV, seq), BF16),
                   jax.ShapeDtypeStruct((tokens, 5 * D_HGRN), F32),
                   jax.ShapeDtypeStruct((tokens, 2 * D_CONV), F32)],
        compiler_params=_params(1),
        name="inproj",
    )(x2, nw, w_in, qnw, knw, cq, sq, ck, sk, segq, segk)


def _attn_kernel(q_ref, kp_ref, va_ref, nw_ref, o_ref, st_ref, yt_ref):
    n_items = ATTN_SUBTILES * N_HEADS

    def scores_t(it):
        sub, hd = divmod(it, N_HEADS)
        g = hd // (N_HEADS // N_KV_HEADS)
        qp = q_ref[(hd // 2) * LANES:(hd // 2 + 1) * LANES, sub * TQ_ATTN:(sub + 1) * TQ_ATTN]
        st_ref[it % ATTN_SLOTS] = jnp.dot(kp_ref[2 * g + hd % 2], qp, preferred_element_type=F32)

    for it in range(ATTN_AHEAD):
        scores_t(it)
    for it in range(n_items):
        sub, hd = divmod(it, N_HEADS)
        g = hd // (N_HEADS // N_KV_HEADS)
        if it + ATTN_AHEAD < n_items:
            scores_t(it + ATTN_AHEAD)
        slot = it % ATTN_SLOTS
        m = jnp.max(st_ref[slot], axis=0, keepdims=True)
        p = jnp.exp2(st_ref[slot] - m)
        den = jnp.sum(p, axis=0, keepdims=True)
        num = jnp.dot(va_ref[g * HEAD_DIM:(g + 1) * HEAD_DIM, :], p.astype(BF16),
                      preferred_element_type=F32)
        yt_ref[hd * HEAD_DIM:(hd + 1) * HEAD_DIM, :] = num / den
        if hd == N_HEADS - 1:
            y = yt_ref[...].T
            ms = jnp.mean(y * y, axis=-1, keepdims=True)
            o_ref[sub * TQ_ATTN:(sub + 1) * TQ_ATTN, :] = (y * lax.rsqrt(ms + EPS) * nw_ref[...]).astype(BF16)


def _attention(q, kp, va, nw, batch, seq):
    tokens = batch * seq
    rows = ATTN_SUBTILES * TQ_ATTN
    nq = seq // rows
    return pl.pallas_call(
        _attn_kernel,
        grid=(batch, nq),
        in_specs=[pl.BlockSpec((None, D_ATTN, rows), lambda b, j: (b, 0, j)),
                  pl.BlockSpec((2 * N_KV_HEADS, seq, D_KV), lambda b, j: (0, b, 0)),
                  pl.BlockSpec((None, D_KV, seq), lambda b, j: (b, 0, 0)),
                  pl.BlockSpec((1, D_ATTN), lambda b, j: (0, 0))],
        out_specs=pl.BlockSpec((rows, D_ATTN), lambda b, j: (b * nq + j, 0)),
        out_shape=jax.ShapeDtypeStruct((tokens, D_ATTN), BF16),
        scratch_shapes=[pltpu.VMEM((ATTN_SLOTS, seq, TQ_ATTN), F32), pltpu.VMEM((D_ATTN, TQ_ATTN), F32)],
        compiler_params=_params(2),
        name="attn",
    )(q, kp, va, nw)


HG_LEVELS = (32, 16, 8)


def _hgrn_constants():
    n = HG_CHUNK
    idx = np.arange(n)
    fwd = (idx[None, :] <= idx[:, None]).astype(np.float32)
    bwd = fwd[::-1, ::-1]
    m2 = np.stack([np.tile(fwd, (1, 2)), np.tile(bwd, (1, 2))])

    lev = np.full((n, n), len(HG_LEVELS) + 1, np.int32)
    for t in range(n):
        for s in range(t + 1):
            if t // HG_FINE == s // HG_FINE:
                lev[t, s] = len(HG_LEVELS)
            else:
                for li, half in enumerate(HG_LEVELS):
                    if t // (2 * half) == s // (2 * half) and t // half != s // half:
                        lev[t, s] = li
                        break
    lev2 = np.stack([np.tile(lev, (1, HGRN_HEADS)), np.tile(lev[::-1, ::-1], (1, HGRN_HEADS))])
    head = np.arange(D_HGRN) // HEAD_DIM
    headmask = (head[:, None] == head[None, :]).astype(np.float32)
    return m2, lev2, headmask


def _hgrn_kernel(hq_ref, zf_ref, zb_ref, hi_ref, gate_ref, logit_ref, gnw_ref, m2_ref, lev_ref,
                 hmb_ref, seg_ref, o_ref, ofw_ref, obw_ref, st_ref, *, layer):
    seq = hq_ref.shape[0]
    n_chunks = seq // HG_CHUNK
    n = HG_CHUNK

    lg = [logit_ref[i] for i in range(DEPTH)]
    mx = functools.reduce(jnp.maximum, lg)
    ex = [jnp.exp(v - mx) for v in lg]
    den = functools.reduce(lambda a, b: a + b, ex)
    sm = [e / den for e in ex]
    acc = sm[0]
    for i in range(1, layer + 1):
        acc = acc + sm[i]
    lb = acc - sm[0]

    def stage_a(d, z_ref, lb_row, r0):
        z = z_ref[pl.ds(r0, n), :]
        f = lb_row + (1.0 - lb_row) * jax.nn.sigmoid(z)
        g = jnp.log(jnp.maximum(f, F_MIN)) * LOG2_E
        kk = 1.0 - f
        g1 = g.astype(BF16)
        g2 = (g - g1.astype(F32)).astype(BF16)
        gcat = jnp.concatenate([g1, g2], axis=0)
        e_cum = jnp.dot(m2_ref[d], gcat, preferred_element_type=F32)
        return kk, e_cum

    def level_exponent(e_cum, d, li):
        fine = li == len(HG_LEVELS)
        blk = HG_FINE if fine else HG_LEVELS[li]
        parts = []
        for r in range(0, n, blk):
            p_start = r if d == 0 else n - blk - r
            if fine:
                p_ref, later = p_start + HG_REF, True
            else:
                p_ref = (p_start // (2 * blk)) * 2 * blk + blk
                later = p_start >= p_ref
            t_ref = p_ref if d == 0 else n - 1 - p_ref
            ref = e_cum[t_ref:t_ref + 1]
            rows = e_cum[r:r + blk]
            parts.append(rows - ref if later else ref - rows)
        return jnp.concatenate(parts, axis=0)

    def stage_b(d, r0, kk, e_cum):
        hm_bf = hmb_ref[...]
        q = hq_ref[pl.ds(r0, n), :]
        v = hi_ref[pl.ds(r0, n), :]
        lev = lev_ref[d]
        a = jnp.zeros((n, D_HGRN), F32)
        for li in range(len(HG_LEVELS) + 1):
            e_lev = level_exponent(e_cum, d, li)
            x_q = jnp.exp2(e_lev)
            x_k = x_q if li < len(HG_LEVELS) else jnp.exp2(-e_lev)
            qt = (q * x_q).astype(BF16)
            kt = (kk * x_k).astype(BF16)
            kbd = jnp.concatenate([kt] * HGRN_HEADS, axis=0) * hm_bf
            sc = lax.dot_general(qt, kbd, (((1,), (1,)), ((), ())), preferred_element_type=F32)
            a = jnp.where(lev == li, sc, a)
        return a.astype(BF16)

    def stage_b2(d, r0, kk, e_cum, a):
        hm_bf = hmb_ref[...]
        q = hq_ref[pl.ds(r0, n), :]
        v = hi_ref[pl.ds(r0, n), :]
        vbd = jnp.concatenate([v.astype(BF16)] * HGRN_HEADS, axis=0) * hm_bf
        o_intra = jnp.dot(a, vbd, preferred_element_type=F32)

        e_tot = e_cum[n - 1:n] if d == 0 else e_cum[0:1]
        qb = (q * jnp.exp2(e_cum)).astype(BF16)
        kdec = kk * jnp.exp2(e_tot - e_cum)
        zpad = jnp.zeros((LANES - n - SUBLANES, D_HGRN), F32)
        tot8 = jnp.broadcast_to(e_tot, (SUBLANES, D_HGRN))
        kt = jnp.concatenate([kdec, tot8, zpad], axis=0).T
        decay_col = jnp.exp2(kt[:, n:n + 1])
        vpad = jnp.concatenate([v.astype(BF16), jnp.zeros((LANES - n, D_HGRN), BF16)], axis=0)
        upd = jnp.dot(kt.astype(BF16), vpad, preferred_element_type=F32)
        return o_intra, qb, upd, decay_col

    def stage_c(d, o_intra, qb, upd, decay_col):
        st = st_ref[d]
        o = o_intra + jnp.dot(qb, st.astype(BF16) * hmb_ref[...], preferred_element_type=F32)
        st_ref[d] = st * decay_col + upd
        return o

    st_ref[...] = jnp.zeros_like(st_ref)

    def scan_body(ci, carry):
        items = []
        for u in range(HG_UNROLL):
            cf = ci * HG_UNROLL + u
            items.append((0, zf_ref, lb[0:1], pl.multiple_of(cf * n, n), ofw_ref))
            items.append((1, zb_ref, lb[1:2], pl.multiple_of((n_chunks - 1 - cf) * n, n), obw_ref))
        sa = [stage_a(d, z_ref, lb_row, r0) for d, z_ref, lb_row, r0, _ in items]
        sb = [stage_b(d, r0, *a) for (d, _, _, r0, _), a in zip(items, sa)]
        sb2 = [stage_b2(d, r0, *a, b) for (d, _, _, r0, _), a, b in zip(items, sa, sb)]
        for (d, _, _, r0, out_ref), b in zip(items, sb2):
            out_ref[pl.ds(r0, n), :] = stage_c(d, *b)
        return carry

    lax.fori_loop(0, n_chunks // HG_UNROLL, scan_body, 0)

    gnw = gnw_ref[...]

    def finish_body(i, carry):
        r0 = pl.multiple_of(i * HG_FINISH_ROWS, HG_FINISH_ROWS)
        o = ofw_ref[pl.ds(r0, HG_FINISH_ROWS), :] + obw_ref[pl.ds(r0, HG_FINISH_ROWS), :]
        on = o * lax.rsqrt(_segment_mean_sq(o, seg_ref, HEAD_DIM) + EPS) * gnw
        gate = gate_ref[pl.ds(r0, HG_FINISH_ROWS), :]
        o_ref[pl.ds(r0, HG_FINISH_ROWS), :] = (on * (gate * jax.nn.sigmoid(gate))).astype(BF16)
        return carry

    lax.fori_loop(0, seq // HG_FINISH_ROWS, finish_body, 0, unroll=2)


def _hgrn(hg, logits, gnw, consts, seg, layer, batch, seq):
    tokens = hg.shape[0]
    m2, lev, hmb = consts
    col = lambda j: pl.BlockSpec((seq, D_HGRN), lambda b: (b, j))
    return pl.pallas_call(
        functools.partial(_hgrn_kernel, layer=layer),
        grid=(batch,),
        in_specs=[col(0), col(1), col(2), col(3), col(4),
                  _full(logits.shape), _full((1, D_HGRN)), _full(m2.shape), _full(lev.shape),
                  _full(hmb.shape), _full(seg.shape)],
        out_specs=pl.BlockSpec((seq, D_HGRN), lambda b: (b, 0)),
        out_shape=jax.ShapeDtypeStruct((tokens, D_HGRN), BF16),
        scratch_shapes=[pltpu.VMEM((seq, D_HGRN), F32), pltpu.VMEM((seq, D_HGRN), F32),
                        pltpu.VMEM((2, D_HGRN, D_HGRN), F32)],
        compiler_params=_params(1),
        name="hgrn",
    )(hg, hg, hg, hg, hg, logits, gnw, m2, lev, hmb, seg)


def _conv_kernel(a_ref, b_ref, dww_ref, dwb_ref, lnw_ref, lnb_ref, pww_ref, pwb_ref, nw_ref,
                 o_ref, pad_ref, dw_ref):
    seq = a_ref.shape[0]
    halo = jnp.zeros((CONV_HALO, D_CONV), F32)
    pad_ref[0:CONV_HALO, :] = halo
    pad_ref[CONV_HALO + seq:CONV_HALO + seq + CONV_HALO, :] = halo
    pad_ref[CONV_HALO:CONV_HALO + seq, :] = a_ref[...] * jax.nn.sigmoid(b_ref[...])

    win_rows = CONV_ROWS + 2 * CONV_HALO

    def tap_body(i, carry):
        r0 = pl.multiple_of(i * CONV_ROWS, CONV_ROWS)
        win = pad_ref[pl.ds(r0, win_rows), :]
        acc = jnp.zeros((CONV_ROWS, D_CONV), F32)
        for r in range(SUBLANES):
            shifted = win if r == 0 else pltpu.roll(win, win_rows - r, 0)
            for a8 in range(0, 2 * CONV_HALO, SUBLANES):
                tap = a8 + r - (CONV_HALO - CONV_PAD)
                if 0 <= tap < CONV_WIDTH:
                    acc = acc + shifted[a8:a8 + CONV_ROWS] * dww_ref[tap:tap + 1, :]
        dw_ref[pl.ds(r0, CONV_ROWS), :] = acc + dwb_ref[...]
        return carry

    lax.fori_loop(0, seq // CONV_ROWS, tap_body, 0, unroll=2)

    def post_body(i, carry):
        r0 = pl.multiple_of(i * CONV_POST_ROWS, CONV_POST_ROWS)
        u = dw_ref[pl.ds(r0, CONV_POST_ROWS), :]
        mu = jnp.mean(u, axis=-1, keepdims=True)
        uc = u - mu
        var = jnp.mean(uc * uc, axis=-1, keepdims=True)
        y = uc * lax.rsqrt(var + LN_EPS) * lnw_ref[...] + lnb_ref[...]
        y = y * jax.nn.sigmoid(y)
        y = jnp.dot(y.astype(BF16), pww_ref[...], preferred_element_type=F32) + pwb_ref[...]
        ms = jnp.mean(y * y, axis=-1, keepdims=True)
        o_ref[pl.ds(r0, CONV_POST_ROWS), :] = (y * lax.rsqrt(ms + EPS) * nw_ref[...]).astype(BF16)
        return carry

    lax.fori_loop(0, seq // CONV_POST_ROWS, post_body, 0, unroll=2)


def _conv(cv, dww, dwb, lnw, lnb, pww, pwb, nw, batch, seq):
    tokens = cv.shape[0]
    col = lambda j: pl.BlockSpec((seq, D_CONV), lambda b: (b, j))
    vec = _full((1, D_CONV))
    return pl.pallas_call(
        _conv_kernel,
        grid=(batch,),
        in_specs=[col(0), col(1), _full(dww.shape), vec, vec, vec, _full((D_CONV, D_CONV)), vec, vec],
        out_specs=pl.BlockSpec((seq, D_CONV), lambda b: (b, 0)),
        out_shape=jax.ShapeDtypeStruct((tokens, D_CONV), BF16),
        scratch_shapes=[pltpu.VMEM((seq + 2 * CONV_HALO, D_CONV), F32), pltpu.VMEM((seq, D_CONV), F32)],
        compiler_params=_params(1),
        name="conv",
    )(cv, cv, dww, dwb, lnw, lnb, pww, pwb, nw)


def _ffn_kernel(x_ref, ya_ref, yh_ref, yc_ref, wo_ref, nw_ref, wg_ref, wu_ref, wd_ref,
                o_ref, act_ref):
    x1 = x_ref[...]
    x1 = x1 + jnp.dot(ya_ref[...], wo_ref[0:D_ATTN, :], preferred_element_type=F32)
    x1 = x1 + jnp.dot(yh_ref[...], wo_ref[D_ATTN:D_ATTN + D_HGRN, :], preferred_element_type=F32)
    x1 = x1 + jnp.dot(yc_ref[...], wo_ref[D_ATTN + D_HGRN:D_MODEL, :], preferred_element_type=F32)
    ms = jnp.mean(x1 * x1, axis=-1, keepdims=True)
    h = (x1 * lax.rsqrt(ms + EPS) * nw_ref[...]).astype(BF16)
    for c in range(0, D_FF, FF_CHUNK):
        g = jnp.dot(h, wg_ref[:, c:c + FF_CHUNK], preferred_element_type=F32)
        u = jnp.dot(h, wu_ref[:, c:c + FF_CHUNK], preferred_element_type=F32)
        act_ref[:, c:c + FF_CHUNK] = (g * jax.nn.sigmoid(g) * u).astype(BF16)
    o_ref[...] = x1 + jnp.dot(act_ref[...], wd_ref[...], preferred_element_type=F32)


def _ffn(x2, ya, yh, yc, wo, nw, wg, wu, wd, layer):
    tokens = x2.shape[0]
    row = lambda w: pl.BlockSpec((TM_FFN, w), lambda i: (i, 0))
    resident = lambda shape: _layer_block(shape, layer, pipeline_mode=pl.Buffered(1))
    return pl.pallas_call(
        _ffn_kernel,
        grid=(tokens // TM_FFN,),
        in_specs=[row(D_MODEL), row(D_ATTN), row(D_HGRN), row(D_CONV),
                  resident((D_MODEL, D_MODEL)), _full((1, D_MODEL)),
                  resident((D_MODEL, D_FF)), resident((D_MODEL, D_FF)), resident((D_FF, D_MODEL))],
        out_specs=row(D_MODEL),
        out_shape=jax.ShapeDtypeStruct((tokens, D_MODEL), F32),
        scratch_shapes=[pltpu.VMEM((TM_FFN, D_FF), BF16)],
        compiler_params=_params(1),
        name="ffn",
    )(x2, ya, yh, yc, wo, nw, wg, wu, wd)


def _rope_tables(seq):
    rows = seq // GRID_W
    row_id = jnp.repeat(jnp.arange(rows, dtype=F32), GRID_W)
    col_id = jnp.tile(jnp.arange(GRID_W, dtype=F32), rows)
    half = HEAD_DIM // 2
    inv_freq = ROPE_THETA ** (-jnp.arange(0, half, 2, dtype=F32) / half)
    ang_r = row_id[:, None] * inv_freq[None, :]
    ang_c = col_id[:, None] * inv_freq[None, :]
    ang = jnp.concatenate([ang_r, ang_r, ang_c, ang_c], axis=-1)
    cos, sin = jnp.cos(ang), jnp.sin(ang)
    lane = np.arange(HEAD_DIM)
    sign = jnp.asarray(np.where((lane % (HEAD_DIM // 2)) < HEAD_DIM // 4, -1.0, 1.0), F32)
    sin_signed = sin * sign[None, :]
    scale = HEAD_DIM ** -0.5 * np.log2(np.e)
    return (jnp.tile(cos, (1, N_HEADS)) * scale, jnp.tile(sin_signed, (1, N_HEADS)) * scale,
            jnp.tile(cos, (1, N_KV_HEADS)), jnp.tile(sin_signed, (1, N_KV_HEADS)))


def _segment_matrix(width, seg_len):
    seg = np.arange(width) // seg_len
    return jnp.asarray((seg[:, None] == seg[None, :]).astype(np.float32), BF16)


def kernel(x, mix_norm_w, w_in, q_norm_w, k_norm_w, hgrn_lb_logits, hgrn_gnorm_w, conv_dw_w, conv_dw_b,
           conv_ln_w, conv_ln_b, conv_pw_w, conv_pw_b, attn_out_norm_w, conv_out_norm_w, w_out,
           ffn_norm_w, w_gate, w_up, w_down):
    batch, seq, d_model = x.shape
    assert d_model == D_MODEL and seq % TM_PROJ == 0 and seq % HG_CHUNK == 0
    assert seq % (ATTN_SUBTILES * TQ_ATTN) == 0 and seq % CONV_POST_ROWS == 0 and seq % HG_FINISH_ROWS == 0
    assert (batch * seq) % TM_FFN == 0 and w_in.shape == (DEPTH, D_MODEL, D_IN_PROJ)
    assert w_gate.shape == (DEPTH, D_MODEL, D_FF)

    tabs = _rope_tables(seq)
    segq = _segment_matrix(D_ATTN, HEAD_DIM)
    segk = _segment_matrix(D_KV, HEAD_DIM)
    segh = _segment_matrix(D_HGRN, HEAD_DIM)
    m2, lev, hm = _hgrn_constants()
    hconsts = (jnp.asarray(m2, BF16), jnp.asarray(lev), jnp.asarray(hm, BF16))
    logits = hgrn_lb_logits.astype(F32)

    w_in_b, w_out_b = w_in.astype(BF16), w_out.astype(BF16)
    w_gate_b, w_up_b, w_down_b = w_gate.astype(BF16), w_up.astype(BF16), w_down.astype(BF16)

    x2 = x.reshape(batch * seq, D_MODEL)
    for l in range(DEPTH):
        q, kp, va, hg, cv = _inproj(
            x2, mix_norm_w[l][None, :], w_in_b,
            jnp.tile(q_norm_w[l], N_HEADS)[None, :], jnp.tile(k_norm_w[l], N_KV_HEADS)[None, :],
            tabs, segq, segk, l, batch, seq)
        ya = _attention(q, kp, va, attn_out_norm_w[l][None, :], batch, seq)
        yh = _hgrn(hg, logits, jnp.tile(hgrn_gnorm_w[l], HGRN_HEADS)[None, :], hconsts, segh, l, batch, seq)
        yc = _conv(cv, conv_dw_w[l], conv_dw_b[l][None, :], conv_ln_w[l][None, :], conv_ln_b[l][None, :],
                   conv_pw_w[l].astype(BF16), conv_pw_b[l][None, :], conv_out_norm_w[l][None, :], batch, seq)
        x2 = _ffn(x2, ya, yh, yc, w_out_b, ffn_norm_w[l][None, :], w_gate_b, w_up_b, w_down_b, l)
    return x2.reshape(batch, seq, D_MODEL)
```

```python
import functools

import numpy as np
import jax
import jax.numpy as jnp
from jax import lax
from jax.experimental import pallas as pl
from jax.experimental.pallas import tpu as pltpu

F32 = jnp.float32
BF16 = jnp.bfloat16

D_MODEL = 1024
DEPTH = 2
GRID_W = 64
D_ATTN = 512
D_HGRN = 256
D_CONV = 256
HEAD_DIM = 64
N_HEADS = D_ATTN // HEAD_DIM
N_KV_HEADS = 2
D_KV = N_KV_HEADS * HEAD_DIM
ROPE_THETA = 10000.0
HGRN_HEADS = D_HGRN // HEAD_DIM
F_MIN = 1e-6
LOG2_E = float(np.log2(np.e))
CONV_WIDTH = 31
CONV_PAD = (CONV_WIDTH - 1) // 2
D_FF = 2816
EPS = 1e-6
LN_EPS = 1e-5
D_IN_PROJ = D_ATTN + 2 * D_KV + 5 * D_HGRN + 2 * D_CONV

OFF_Q = 0
OFF_K = OFF_Q + D_ATTN
OFF_V = OFF_K + D_KV
OFF_H = OFF_V + D_KV
OFF_C = OFF_H + 5 * D_HGRN

LANES = 128
SUBLANES = 8
VMEM_LIMIT_BYTES = 56 * 1024 * 1024

TM_PROJ = 1024
TQ_ATTN = 256
ATTN_SUBTILES = 4
ATTN_AHEAD = 3
ATTN_SLOTS = ATTN_AHEAD + 1
TM_FFN = 1024
FF_CHUNK = 256
HG_CHUNK = 64
HG_FINE = 8
HG_REF = 2
HG_FINISH_ROWS = 256
HG_UNROLL = 4
CONV_ROWS = 128
CONV_POST_ROWS = 256
CONV_HALO = 16

MAX_SAFE_EXP = 80.0
assert -np.log(F_MIN) * max(HG_FINE - 1 - HG_REF, HG_REF) < MAX_SAFE_EXP


def _params(n_grid_axes):
    return pltpu.CompilerParams(
        dimension_semantics=("arbitrary",) * n_grid_axes,
        vmem_limit_bytes=VMEM_LIMIT_BYTES)


def _full(shape):
    nd = len(shape)
    return pl.BlockSpec(shape, lambda *_: (0,) * nd)


def _segment_mean_sq(y, seg_ref, seg_len):
    return jnp.dot((y * y).astype(BF16), seg_ref[...], preferred_element_type=F32) * (1.0 / seg_len)


def _rope(xn, cos, sin_signed, first_mask):
    width = xn.shape[1]
    left = pltpu.roll(xn, width - HEAD_DIM // 4, 1)
    right = pltpu.roll(xn, HEAD_DIM // 4, 1)
    rot = jnp.where(first_mask, left, right)
    return xn * cos + rot * sin_signed


def _inproj_kernel(x_ref, nw_ref, w_ref, qnw_ref, knw_ref, cq_ref, sq_ref, ck_ref, sk_ref,
                   segq_ref, segk_ref,
                   q_out, kp_out, va_out, hg_out, cv_out):
    x = x_ref[...]
    ms = jnp.mean(x * x, axis=-1, keepdims=True)
    h = (x * lax.rsqrt(ms + EPS) * nw_ref[...]).astype(BF16)

    q = jnp.dot(h, w_ref[:, OFF_Q:OFF_K], preferred_element_type=F32)
    qn = q * lax.rsqrt(_segment_mean_sq(q, segq_ref, HEAD_DIM) + EPS) * qnw_ref[...]
    lane_q = lax.broadcasted_iota(jnp.int32, (1, D_ATTN), 1)
    first_q = (lane_q & (HEAD_DIM // 2 - 1)) < HEAD_DIM // 4
    q_out[...] = _rope(qn, cq_ref[...], sq_ref[...], first_q).T.astype(BF16)

    k = jnp.dot(h, w_ref[:, OFF_K:OFF_V], preferred_element_type=F32)
    kn = k * lax.rsqrt(_segment_mean_sq(k, segk_ref, HEAD_DIM) + EPS) * knw_ref[...]
    lane_k = lax.broadcasted_iota(jnp.int32, (1, D_KV), 1)
    first_k = (lane_k & (HEAD_DIM // 2 - 1)) < HEAD_DIM // 4
    kr = _rope(kn, ck_ref[...], sk_ref[...], first_k)
    ks = pltpu.roll(kr, HEAD_DIM, 1)
    low = lane_k < HEAD_DIM
    zero = jnp.zeros_like(kr)
    kp_out[0] = jnp.where(low, kr, zero).astype(BF16)
    kp_out[1] = jnp.where(low, zero, ks).astype(BF16)
    kp_out[2] = jnp.where(low, ks, zero).astype(BF16)
    kp_out[3] = jnp.where(low, zero, kr).astype(BF16)

    v = jnp.dot(h, w_ref[:, OFF_V:OFF_H], preferred_element_type=F32)
    va_out[...] = v.T.astype(BF16)

    hg_out[...] = jnp.dot(h, w_ref[:, OFF_H:OFF_C], preferred_element_type=F32)
    cv_out[...] = jnp.dot(h, w_ref[:, OFF_C:D_IN_PROJ], preferred_element_type=F32)


def _layer_block(shape, layer, **kwargs):
    return pl.BlockSpec((None,) + tuple(shape), lambda *_: (layer,) + (0,) * len(shape), **kwargs)


def _inproj(x2, nw, w_in, qnw, knw, tabs, segq, segk, layer, batch, seq):
    tokens = x2.shape[0]
    n_seq_tiles = seq // TM_PROJ
    cq, sq, ck, sk = tabs
    tab_q = pl.BlockSpec((TM_PROJ, D_ATTN), lambda i: (i % n_seq_tiles, 0))
    tab_k = pl.BlockSpec((TM_PROJ, D_KV), lambda i: (i % n_seq_tiles, 0))
    row = lambda w: pl.BlockSpec((TM_PROJ, w), lambda i: (i, 0))
    return pl.pallas_call(
        _inproj_kernel,
        grid=(tokens // TM_PROJ,),
        in_specs=[row(D_MODEL), _full((1, D_MODEL)), _layer_block((D_MODEL, D_IN_PROJ), layer),
                  _full((1, D_ATTN)), _full((1, D_KV)), tab_q, tab_q, tab_k, tab_k,
                  _full((D_ATTN, D_ATTN)), _full((D_KV, D_KV))],
        out_specs=[pl.BlockSpec((None, D_ATTN, TM_PROJ), lambda i: (i // n_seq_tiles, 0, i % n_seq_tiles)),
                   pl.BlockSpec((2 * N_KV_HEADS, TM_PROJ, D_KV), lambda i: (0, i, 0)),
                   pl.BlockSpec((None, D_KV, TM_PROJ), lambda i: (i // n_seq_tiles, 0, i % n_seq_tiles)),
                   row(5 * D_HGRN), row(2 * D_CONV)],
        out_shape=[jax.ShapeDtypeStruct((batch, D_ATTN, seq), BF16),
                   jax.ShapeDtypeStruct((2 * N_KV_HEADS, tokens, D_KV), BF16),
                   jax.ShapeDtypeStruct((batch, D_KV, seq), BF16),
                   jax.ShapeDtypeStruct((tokens, 5 * D_HGRN), F32),
                   jax.ShapeDtypeStruct((tokens, 2 * D_CONV), F32)],
        compiler_params=_params(1),
        name="inproj",
    )(x2, nw, w_in, qnw, knw, cq, sq, ck, sk, segq, segk)


def _attn_kernel(q_ref, kp_ref, va_ref, nw_ref, o_ref, st_ref, yt_ref):
    n_items = ATTN_SUBTILES * N_HEADS

    def scores_t(it):
        sub, hd = divmod(it, N_HEADS)
        g = hd // (N_HEADS // N_KV_HEADS)
        qp = q_ref[(hd // 2) * LANES:(hd // 2 + 1) * LANES, sub * TQ_ATTN:(sub + 1) * TQ_ATTN]
        st_ref[it % ATTN_SLOTS] = jnp.dot(kp_ref[2 * g + hd % 2], qp, preferred_element_type=F32)

    for it in range(ATTN_AHEAD):
        scores_t(it)
    for it in range(n_items):
        sub, hd = divmod(it, N_HEADS)
        g = hd // (N_HEADS // N_KV_HEADS)
        if it + ATTN_AHEAD < n_items:
            scores_t(it + ATTN_AHEAD)
        slot = it % ATTN_SLOTS
        m = jnp.max(st_ref[slot], axis=0, keepdims=True)
        p = jnp.exp2(st_ref[slot] - m)
        den = jnp.sum(p, axis=0, keepdims=True)
        num = jnp.dot(va_ref[g * HEAD_DIM:(g + 1) * HEAD_DIM, :], p.astype(BF16),
                      preferred_element_type=F32)
        yt_ref[hd * HEAD_DIM:(hd + 1) * HEAD_DIM, :] = num / den
        if hd == N_HEADS - 1:
            y = yt_ref[...].T
            ms = jnp.mean(y * y, axis=-1, keepdims=True)
            o_ref[sub * TQ_ATTN:(sub + 1) * TQ_ATTN, :] = (y * lax.rsqrt(ms + EPS) * nw_ref[...]).astype(BF16)


def _attention(q, kp, va, nw, batch, seq):
    tokens = batch * seq
    rows = ATTN_SUBTILES * TQ_ATTN
    nq = seq // rows
    return pl.pallas_call(
        _attn_kernel,
        grid=(batch, nq),
        in_specs=[pl.BlockSpec((None, D_ATTN, rows), lambda b, j: (b, 0, j)),
                  pl.BlockSpec((2 * N_KV_HEADS, seq, D_KV), lambda b, j: (0, b, 0)),
                  pl.BlockSpec((None, D_KV, seq), lambda b, j: (b, 0, 0)),
                  pl.BlockSpec((1, D_ATTN), lambda b, j: (0, 0))],
        out_specs=pl.BlockSpec((rows, D_ATTN), lambda b, j: (b * nq + j, 0)),
        out_shape=jax.ShapeDtypeStruct((tokens, D_ATTN), BF16),
        scratch_shapes=[pltpu.VMEM((ATTN_SLOTS, seq, TQ_ATTN), F32), pltpu.VMEM((D_ATTN, TQ_ATTN), F32)],
        compiler_params=_params(2),
        name="attn",
    )(q, kp, va, nw)


HG_LEVELS = (32, 16, 8)


def _hgrn_constants():
    n = HG_CHUNK
    idx = np.arange(n)
    fwd = (idx[None, :] <= idx[:, None]).astype(np.float32)
    bwd = fwd[::-1, ::-1]
    m2 = np.stack([np.tile(fwd, (1, 2)), np.tile(bwd, (1, 2))])

    lev = np.full((n, n), len(HG_LEVELS) + 1, np.int32)
    for t in range(n):
        for s in range(t + 1):
            if t // HG_FINE == s // HG_FINE:
                lev[t, s] = len(HG_LEVELS)
            else:
                for li, half in enumerate(HG_LEVELS):
                    if t // (2 * half) == s // (2 * half) and t // half != s // half:
                        lev[t, s] = li
                        break
    lev2 = np.stack([np.tile(lev, (1, HGRN_HEADS)), np.tile(lev[::-1, ::-1], (1, HGRN_HEADS))])
    head = np.arange(D_HGRN) // HEAD_DIM
    headmask = (head[:, None] == head[None, :]).astype(np.float32)
    return m2, lev2, headmask


def _hgrn_kernel(hq_ref, zf_ref, zb_ref, hi_ref, gate_ref, logit_ref, gnw_ref, m2_ref, lev_ref,
                 hmb_ref, seg_ref, o_ref, ofw_ref, obw_ref, st_ref, *, layer):
    seq = hq_ref.shape[0]
    n_chunks = seq // HG_CHUNK
    n = HG_CHUNK

    lg = [logit_ref[i] for i in range(DEPTH)]
    mx = functools.reduce(jnp.maximum, lg)
    ex = [jnp.exp(v - mx) for v in lg]
    den = functools.reduce(lambda a, b: a + b, ex)
    sm = [e / den for e in ex]
    acc = sm[0]
    for i in range(1, layer + 1):
        acc = acc + sm[i]
    lb = acc - sm[0]

    def stage_a(d, z_ref, lb_row, r0):
        z = z_ref[pl.ds(r0, n), :]
        f = lb_row + (1.0 - lb_row) * jax.nn.sigmoid(z)
        g = jnp.log(jnp.maximum(f, F_MIN)) * LOG2_E
        kk = 1.0 - f
        g1 = g.astype(BF16)
        g2 = (g - g1.astype(F32)).astype(BF16)
        gcat = jnp.concatenate([g1, g2], axis=0)
        e_cum = jnp.dot(m2_ref[d], gcat, preferred_element_type=F32)
        return kk, e_cum

    def level_exponent(e_cum, d, li):
        fine = li == len(HG_LEVELS)
        blk = HG_FINE if fine else HG_LEVELS[li]
        parts = []
        for r in range(0, n, blk):
            p_start = r if d == 0 else n - blk - r
            if fine:
                p_ref, later = p_start + HG_REF, True
            else:
                p_ref = (p_start // (2 * blk)) * 2 * blk + blk
                later = p_start >= p_ref
            t_ref = p_ref if d == 0 else n - 1 - p_ref
            ref = e_cum[t_ref:t_ref + 1]
            rows = e_cum[r:r + blk]
            parts.append(rows - ref if later else ref - rows)
        return jnp.concatenate(parts, axis=0)

    def stage_b(d, r0, kk, e_cum):
        hm_bf = hmb_ref[...]
        q = hq_ref[pl.ds(r0, n), :]
        v = hi_ref[pl.ds(r0, n), :]
        lev = lev_ref[d]
        a = jnp.zeros((n, D_HGRN), F32)
        for li in range(len(HG_LEVELS) + 1):
            e_lev = level_exponent(e_cum, d, li)
            x_q = jnp.exp2(e_lev)
            x_k = x_q if li < len(HG_LEVELS) else jnp.exp2(-e_lev)
            qt = (q * x_q).astype(BF16)
            kt = (kk * x_k).astype(BF16)
            kbd = jnp.concatenate([kt] * HGRN_HEADS, axis=0) * hm_bf
            sc = lax.dot_general(qt, kbd, (((1,), (1,)), ((), ())), preferred_element_type=F32)
            a = jnp.where(lev == li, sc, a)
        return a.astype(BF16)

    def stage_b2(d, r0, kk, e_cum, a):
        hm_bf = hmb_ref[...]
        q = hq_ref[pl.ds(r0, n), :]
        v = hi_ref[pl.ds(r0, n), :]
        vbd = jnp.concatenate([v.astype(BF16)] * HGRN_HEADS, axis=0) * hm_bf
        o_intra = jnp.dot(a, vbd, preferred_element_type=F32)

        e_tot = e_cum[n - 1:n] if d == 0 else e_cum[0:1]
        qb = (q * jnp.exp2(e_cum)).astype(BF16)
        kdec = kk * jnp.exp2(e_tot - e_cum)
        zpad = jnp.zeros((LANES - n - SUBLANES, D_HGRN), F32)
        tot8 = jnp.broadcast_to(e_tot, (SUBLANES, D_HGRN))
        kt = jnp.concatenate([kdec, tot8, zpad], axis=0).T
        decay_col = jnp.exp2(kt[:, n:n + 1])
        vpad = jnp.concatenate([v.astype(BF16), jnp.zeros((LANES - n, D_HGRN), BF16)], axis=0)
        upd = jnp.dot(kt.astype(BF16), vpad, preferred_element_type=F32)
        return o_intra, qb, upd, decay_col

    def stage_c(d, o_intra, qb, upd, decay_col):
        st = st_ref[d]
        o = o_intra + jnp.dot(qb, st.astype(BF16) * hmb_ref[...], preferred_element_type=F32)
        st_ref[d] = st * decay_col + upd
        return o

    st_ref[...] = jnp.zeros_like(st_ref)

    def scan_body(ci, carry):
        items = []
        for u in range(HG_UNROLL):
            cf = ci * HG_UNROLL + u
            items.append((0, zf_ref, lb[0:1], pl.multiple_of(cf * n, n), ofw_ref))
            items.append((1, zb_ref, lb[1:2], pl.multiple_of((n_chunks - 1 - cf) * n, n), obw_ref))
        sa = [stage_a(d, z_ref, lb_row, r0) for d, z_ref, lb_row, r0, _ in items]
        sb = [stage_b(d, r0, *a) for (d, _, _, r0, _), a in zip(items, sa)]
        sb2 = [stage_b2(d, r0, *a, b) for (d, _, _, r0, _), a, b in zip(items, sa, sb)]
        for (d, _, _, r0, out_ref), b in zip(items, sb2):
            out_ref[pl.ds(r0, n), :] = stage_c(d, *b)
        return carry

    lax.fori_loop(0, n_chunks // HG_UNROLL, scan_body, 0)

    gnw = gnw_ref[...]

    def finish_body(i, carry):
        r0 = pl.multiple_of(i * HG_FINISH_ROWS, HG_FINISH_ROWS)
        o = ofw_ref[pl.ds(r0, HG_FINISH_ROWS), :] + obw_ref[pl.ds(r0, HG_FINISH_ROWS), :]
        on = o * lax.rsqrt(_segment_mean_sq(o, seg_ref, HEAD_DIM) + EPS) * gnw
        gate = gate_ref[pl.ds(r0, HG_FINISH_ROWS), :]
        o_ref[pl.ds(r0, HG_FINISH_ROWS), :] = (on * (gate * jax.nn.sigmoid(gate))).astype(BF16)
        return carry

    lax.fori_loop(0, seq // HG_FINISH_ROWS, finish_body, 0, unroll=2)


def _hgrn(hg, logits, gnw, consts, seg, layer, batch, seq):
    tokens = hg.shape[0]
    m2, lev, hmb = consts
    col = lambda j: pl.BlockSpec((seq, D_HGRN), lambda b: (b, j))
    return pl.pallas_call(
        functools.partial(_hgrn_kernel, layer=layer),
        grid=(batch,),
        in_specs=[col(0), col(1), col(2), col(3), col(4),
                  _full(logits.shape), _full((1, D_HGRN)), _full(m2.shape), _full(lev.shape),
                  _full(hmb.shape), _full(seg.shape)],
        out_specs=pl.BlockSpec((seq, D_HGRN), lambda b: (b, 0)),
        out_shape=jax.ShapeDtypeStruct((tokens, D_HGRN), BF16),
        scratch_shapes=[pltpu.VMEM((seq, D_HGRN), F32), pltpu.VMEM((seq, D_HGRN), F32),
                        pltpu.VMEM((2, D_HGRN, D_HGRN), F32)],
        compiler_params=_params(1),
        name="hgrn",
    )(hg, hg, hg, hg, hg, logits, gnw, m2, lev, hmb, seg)


def _conv_kernel(a_ref, b_ref, dww_ref, dwb_ref, lnw_ref, lnb_ref, pww_ref, pwb_ref, nw_ref,
                 o_ref, pad_ref, dw_ref):
    seq = a_ref.shape[0]
    halo = jnp.zeros((CONV_HALO, D_CONV), F32)
    pad_ref[0:CONV_HALO, :] = halo
    pad_ref[CONV_HALO + seq:CONV_HALO + seq + CONV_HALO, :] = halo
    pad_ref[CONV_HALO:CONV_HALO + seq, :] = a_ref[...] * jax.nn.sigmoid(b_ref[...])

    win_rows = CONV_ROWS + 2 * CONV_HALO

    def tap_body(i, carry):
        r0 = pl.multiple_of(i * CONV_ROWS, CONV_ROWS)
        win = pad_ref[pl.ds(r0, win_rows), :]
        acc = jnp.zeros((CONV_ROWS, D_CONV), F32)
        for r in range(SUBLANES):
            shifted = win if r == 0 else pltpu.roll(win, win_rows - r, 0)
            for a8 in range(0, 2 * CONV_HALO, SUBLANES):
                tap = a8 + r - (CONV_HALO - CONV_PAD)
                if 0 <= tap < CONV_WIDTH:
                    acc = acc + shifted[a8:a8 + CONV_ROWS] * dww_ref[tap:tap + 1, :]
        dw_ref[pl.ds(r0, CONV_ROWS), :] = acc + dwb_ref[...]
        return carry

    lax.fori_loop(0, seq // CONV_ROWS, tap_body, 0, unroll=2)

    def post_body(i, carry):
        r0 = pl.multiple_of(i * CONV_POST_ROWS, CONV_POST_ROWS)
        u = dw_ref[pl.ds(r0, CONV_POST_ROWS), :]
        mu = jnp.mean(u, axis=-1, keepdims=True)
        uc = u - mu
        var = jnp.mean(uc * uc, axis=-1, keepdims=True)
        y = uc * lax.rsqrt(var + LN_EPS) * lnw_ref[...] + lnb_ref[...]
        y = y * jax.nn.sigmoid(y)
        y = jnp.dot(y.astype(BF16), pww_ref[...], preferred_element_type=F32) + pwb_ref[...]
        ms = jnp.mean(y * y, axis=-1, keepdims=True)
        o_ref[pl.ds(r0, CONV_POST_ROWS), :] = (y * lax.rsqrt(ms + EPS) * nw_ref[...]).astype(BF16)
        return carry

    lax.fori_loop(0, seq // CONV_POST_ROWS, post_body, 0, unroll=2)


def _conv(cv, dww, dwb, lnw, lnb, pww, pwb, nw, batch, seq):
    tokens = cv.shape[0]
    col = lambda j: pl.BlockSpec((seq, D_CONV), lambda b: (b, j))
    vec = _full((1, D_CONV))
    return pl.pallas_call(
        _conv_kernel,
        grid=(batch,),
        in_specs=[col(0), col(1), _full(dww.shape), vec, vec, vec, _full((D_CONV, D_CONV)), vec, vec],
        out_specs=pl.BlockSpec((seq, D_CONV), lambda b: (b, 0)),
        out_shape=jax.ShapeDtypeStruct((tokens, D_CONV), BF16),
        scratch_shapes=[pltpu.VMEM((seq + 2 * CONV_HALO, D_CONV), F32), pltpu.VMEM((seq, D_CONV), F32)],
        compiler_params=_params(1),
        name="conv",
    )(cv, cv, dww, dwb, lnw, lnb, pww, pwb, nw)


def _ffn_kernel(x_ref, ya_ref, yh_ref, yc_ref, wo_ref, nw_ref, wg_ref, wu_ref, wd_ref,
                o_ref, act_ref):
    x1 = x_ref[...]
    x1 = x1 + jnp.dot(ya_ref[...], wo_ref[0:D_ATTN, :], preferred_element_type=F32)
    x1 = x1 + jnp.dot(yh_ref[...], wo_ref[D_ATTN:D_ATTN + D_HGRN, :], preferred_element_type=F32)
    x1 = x1 + jnp.dot(yc_ref[...], wo_ref[D_ATTN + D_HGRN:D_MODEL, :], preferred_element_type=F32)
    ms = jnp.mean(x1 * x1, axis=-1, keepdims=True)
    h = (x1 * lax.rsqrt(ms + EPS) * nw_ref[...]).astype(BF16)
    for c in range(0, D_FF, FF_CHUNK):
        g = jnp.dot(h, wg_ref[:, c:c + FF_CHUNK], preferred_element_type=F32)
        u = jnp.dot(h, wu_ref[:, c:c + FF_CHUNK], preferred_element_type=F32)
        act_ref[:, c:c + FF_CHUNK] = (g * jax.nn.sigmoid(g) * u).astype(BF16)
    o_ref[...] = x1 + jnp.dot(act_ref[...], wd_ref[...], preferred_element_type=F32)


def _ffn(x2, ya, yh, yc, wo, nw, wg, wu, wd, layer):
    tokens = x2.shape[0]
    row = lambda w: pl.BlockSpec((TM_FFN, w), lambda i: (i, 0))
    resident = lambda shape: _layer_block(shape, layer, pipeline_mode=pl.Buffered(1))
    return pl.pallas_call(
        _ffn_kernel,
        grid=(tokens // TM_FFN,),
        in_specs=[row(D_MODEL), row(D_ATTN), row(D_HGRN), row(D_CONV),
                  resident((D_MODEL, D_MODEL)), _full((1, D_MODEL)),
                  resident((D_MODEL, D_FF)), resident((D_MODEL, D_FF)), resident((D_FF, D_MODEL))],
        out_specs=row(D_MODEL),
        out_shape=jax.ShapeDtypeStruct((tokens, D_MODEL), F32),
        scratch_shapes=[pltpu.VMEM((TM_FFN, D_FF), BF16)],
        compiler_params=_params(1),
        name="ffn",
    )(x2, ya, yh, yc, wo, nw, wg, wu, wd)


def _rope_tables(seq):
    rows = seq // GRID_W
    row_id = jnp.repeat(jnp.arange(rows, dtype=F32), GRID_W)
    col_id = jnp.tile(jnp.arange(GRID_W, dtype=F32), rows)
    half = HEAD_DIM // 2
    inv_freq = ROPE_THETA ** (-jnp.arange(0, half, 2, dtype=F32) / half)
    ang_r = row_id[:, None] * inv_freq[None, :]
    ang_c = col_id[:, None] * inv_freq[None, :]
    ang = jnp.concatenate([ang_r, ang_r, ang_c, ang_c], axis=-1)
    cos, sin = jnp.cos(ang), jnp.sin(ang)
    lane = np.arange(HEAD_DIM)
    sign = jnp.asarray(np.where((lane % (HEAD_DIM // 2)) < HEAD_DIM // 4, -1.0, 1.0), F32)
    sin_signed = sin * sign[None, :]
    scale = HEAD_DIM ** -0.5 * np.log2(np.e)
    return (jnp.tile(cos, (1, N_HEADS)) * scale, jnp.tile(sin_signed, (1, N_HEADS)) * scale,
            jnp.tile(cos, (1, N_KV_HEADS)), jnp.tile(sin_signed, (1, N_KV_HEADS)))


def _segment_matrix(width, seg_len):
    seg = np.arange(width) // seg_len
    return jnp.asarray((seg[:, None] == seg[None, :]).astype(np.float32), BF16)


def kernel(x, mix_norm_w, w_in, q_norm_w, k_norm_w, hgrn_lb_logits, hgrn_gnorm_w, conv_dw_w, conv_dw_b,
           conv_ln_w, conv_ln_b, conv_pw_w, conv_pw_b, attn_out_norm_w, conv_out_norm_w, w_out,
           ffn_norm_w, w_gate, w_up, w_down):
    batch, seq, d_model = x.shape
    assert d_model == D_MODEL and seq % TM_PROJ == 0 and seq % HG_CHUNK == 0
    assert seq % (ATTN_SUBTILES * TQ_ATTN) == 0 and seq % CONV_POST_ROWS == 0 and seq % HG_FINISH_ROWS == 0
    assert (batch * seq) % TM_FFN == 0 and w_in.shape == (DEPTH, D_MODEL, D_IN_PROJ)
    assert w_gate.shape == (DEPTH, D_MODEL, D_FF)

    tabs = _rope_tables(seq)
    segq = _segment_matrix(D_ATTN, HEAD_DIM)
    segk = _segment_matrix(D_KV, HEAD_DIM)
    segh = _segment_matrix(D_HGRN, HEAD_DIM)
    m2, lev, hm = _hgrn_constants()
    hconsts = (jnp.asarray(m2, BF16), jnp.asarray(lev), jnp.asarray(hm, BF16))
    logits = hgrn_lb_logits.astype(F32)

    w_in_b, w_out_b = w_in.astype(BF16), w_out.astype(BF16)
    w_gate_b, w_up_b, w_down_b = w_gate.astype(BF16), w_up.astype(BF16), w_down.astype(BF16)

    x2 = x.reshape(batch * seq, D_MODEL)
    for l in range(DEPTH):
        q, kp, va, hg, cv = _inproj(
            x2, mix_norm_w[l][None, :], w_in_b,
            jnp.tile(q_norm_w[l], N_HEADS)[None, :], jnp.tile(k_norm_w[l], N_KV_HEADS)[None, :],
            tabs, segq, segk, l, batch, seq)
        ya = _attention(q, kp, va, attn_out_norm_w[l][None, :], batch, seq)
        yh = _hgrn(hg, logits, jnp.tile(hgrn_gnorm_w[l], HGRN_HEADS)[None, :], hconsts, segh, l, batch, seq)
        yc = _conv(cv, conv_dw_w[l], conv_dw_b[l][None, :], conv_ln_w[l][None, :], conv_ln_b[l][None, :],
                   conv_pw_w[l].astype(BF16), conv_pw_b[l][None, :], conv_out_norm_w[l][None, :], batch, seq)
        x2 = _ffn(x2, ya, yh, yc, w_out_b, ffn_norm_w[l][None, :], w_gate_b, w_up_b, w_down_b, l)
    return x2.reshape(batch, seq, D_MODEL)
```

```python
import functools

import numpy as np
import jax
import jax.numpy as jnp
from jax import lax
from jax.experimental import pallas as pl
from jax.experimental.pallas import tpu as pltpu

F32 = jnp.float32
BF16 = jnp.bfloat16

D_MODEL = 1024
DEPTH = 2
GRID_W = 64
D_ATTN = 512
D_HGRN = 256
D_CONV = 256
HEAD_DIM = 64
N_HEADS = D_ATTN // HEAD_DIM
N_KV_HEADS = 2
D_KV = N_KV_HEADS * HEAD_DIM
ROPE_THETA = 10000.0
HGRN_HEADS = D_HGRN // HEAD_DIM
F_MIN = 1e-6
LOG2_E = float(np.log2(np.e))
CONV_WIDTH = 31
CONV_PAD = (CONV_WIDTH - 1) // 2
D_FF = 2816
EPS = 1e-6
LN_EPS = 1e-5
D_IN_PROJ = D_ATTN + 2 * D_KV + 5 * D_HGRN + 2 * D_CONV

OFF_Q = 0
OFF_K = OFF_Q + D_ATTN
OFF_V = OFF_K + D_KV
OFF_H = OFF_V + D_KV
OFF_C = OFF_H + 5 * D_HGRN

LANES = 128
SUBLANES = 8
VMEM_LIMIT_BYTES = 56 * 1024 * 1024

TM_PROJ = 1024
TQ_ATTN = 256
ATTN_SUBTILES = 4
ATTN_AHEAD = 3
ATTN_SLOTS = ATTN_AHEAD + 1
TM_FFN = 1024
FF_CHUNK = 256
HG_CHUNK = 64
HG_FINE = 8
HG_REF = 2
HG_FINISH_ROWS = 256
HG_UNROLL = 4
CONV_ROWS = 128
CONV_POST_ROWS = 256
CONV_HALO = 16

MAX_SAFE_EXP = 80.0
assert -np.log(F_MIN) * max(HG_FINE - 1 - HG_REF, HG_REF) < MAX_SAFE_EXP


def _params(n_grid_axes):
    return pltpu.CompilerParams(
        dimension_semantics=("arbitrary",) * n_grid_axes,
        vmem_limit_bytes=VMEM_LIMIT_BYTES)


def _full(shape):
    nd = len(shape)
    return pl.BlockSpec(shape, lambda *_: (0,) * nd)


def _segment_mean_sq(y, seg_ref, seg_len):
    return jnp.dot((y * y).astype(BF16), seg_ref[...], preferred_element_type=F32) * (1.0 / seg_len)


def _rope(xn, cos, sin_signed, first_mask):
    width = xn.shape[1]
    left = pltpu.roll(xn, width - HEAD_DIM // 4, 1)
    right = pltpu.roll(xn, HEAD_DIM // 4, 1)
    rot = jnp.where(first_mask, left, right)
    return xn * cos + rot * sin_signed


def _inproj_kernel(x_ref, nw_ref, w_ref, qnw_ref, knw_ref, cq_ref, sq_ref, ck_ref, sk_ref,
                   segq_ref, segk_ref,
                   q_out, kp_out, va_out, hg_out, cv_out):
    x = x_ref[...]
    ms = jnp.mean(x * x, axis=-1, keepdims=True)
    h = (x * lax.rsqrt(ms + EPS) * nw_ref[...]).astype(BF16)

    q = jnp.dot(h, w_ref[:, OFF_Q:OFF_K], preferred_element_type=F32)
    qt = q.T
    quarter = HEAD_DIM // 4
    normed = []
    for hd in range(N_HEADS):
        blk = qt[hd * HEAD_DIM:(hd + 1) * HEAD_DIM]
        ss = jnp.sum(blk * blk, axis=0, keepdims=True) * (1.0 / HEAD_DIM)
        normed.append(blk * lax.rsqrt(ss + EPS))
    qn = jnp.concatenate(normed, axis=0) * qnw_ref[...]
    swapped = []
    for hd in range(N_HEADS):
        b = hd * HEAD_DIM
        swapped += [qn[b + quarter:b + 2 * quarter], qn[b:b + quarter],
                    qn[b + 3 * quarter:b + 4 * quarter], qn[b + 2 * quarter:b + 3 * quarter]]
    rot = jnp.concatenate(swapped, axis=0)
    q_out[...] = (qn * cq_ref[...] + rot * sq_ref[...]).astype(BF16)

    k = jnp.dot(h, w_ref[:, OFF_K:OFF_V], preferred_element_type=F32)
    kn = k * lax.rsqrt(_segment_mean_sq(k, segk_ref, HEAD_DIM) + EPS) * knw_ref[...]
    lane_k = lax.broadcasted_iota(jnp.int32, (1, D_KV), 1)
    first_k = (lane_k & (HEAD_DIM // 2 - 1)) < HEAD_DIM // 4
    kr = _rope(kn, ck_ref[...], sk_ref[...], first_k)
    ks = pltpu.roll(kr, HEAD_DIM, 1)
    low = lane_k < HEAD_DIM
    zero = jnp.zeros_like(kr)
    kp_out[0] = jnp.where(low, kr, zero).astype(BF16)
    kp_out[1] = jnp.where(low, zero, ks).astype(BF16)
    kp_out[2] = jnp.where(low, ks, zero).astype(BF16)
    kp_out[3] = jnp.where(low, zero, kr).astype(BF16)

    v = jnp.dot(h, w_ref[:, OFF_V:OFF_H], preferred_element_type=F32)
    va_out[...] = v.T.astype(BF16)

    hg_out[...] = jnp.dot(h, w_ref[:, OFF_H:OFF_C], preferred_element_type=F32)
    cv_out[...] = jnp.dot(h, w_ref[:, OFF_C:D_IN_PROJ], preferred_element_type=F32)


def _layer_block(shape, layer, **kwargs):
    return pl.BlockSpec((None,) + tuple(shape), lambda *_: (layer,) + (0,) * len(shape), **kwargs)


def _inproj(x2, nw, w_in, qnw, knw, tabs, segq, segk, layer, batch, seq):
    tokens = x2.shape[0]
    n_seq_tiles = seq // TM_PROJ
    cq, sq, ck, sk = tabs
    tab_q = pl.BlockSpec((D_ATTN, TM_PROJ), lambda i: (0, i % n_seq_tiles))
    tab_k = pl.BlockSpec((TM_PROJ, D_KV), lambda i: (i % n_seq_tiles, 0))
    row = lambda w: pl.BlockSpec((TM_PROJ, w), lambda i: (i, 0))
    return pl.pallas_call(
        _inproj_kernel,
        grid=(tokens // TM_PROJ,),
        in_specs=[row(D_MODEL), _full((1, D_MODEL)), _layer_block((D_MODEL, D_IN_PROJ), layer),
                  _full((D_ATTN, 1)), _full((1, D_KV)), tab_q, tab_q, tab_k, tab_k,
                  _full((D_ATTN, D_ATTN)), _full((D_KV, D_KV))],
        out_specs=[pl.BlockSpec((None, D_ATTN, TM_PROJ), lambda i: (i // n_seq_tiles, 0, i % n_seq_tiles)),
                   pl.BlockSpec((2 * N_KV_HEADS, TM_PROJ, D_KV), lambda i: (0, i, 0)),
                   pl.BlockSpec((None, D_KV, TM_PROJ), lambda i: (i // n_seq_tiles, 0, i % n_seq_tiles)),
                   row(5 * D_HGRN), row(2 * D_CONV)],
        out_shape=[jax.ShapeDtypeStruct((batch, D_ATTN, seq), BF16),
                   jax.ShapeDtypeStruct((2 * N_KV_HEADS, tokens, D_KV), BF16),
                   jax.ShapeDtypeStruct((batch, D_KV, seq), BF16),
                   jax.ShapeDtypeStruct((tokens, 5 * D_HGRN), F32),
                   jax.ShapeDtypeStruct((tokens, 2 * D_CONV), F32)],
        compiler_params=_params(1),
        name="inproj",
    )(x2, nw, w_in, qnw, knw, cq, sq, ck, sk, segq, segk)


def _attn_kernel(q_ref, kp_ref, va_ref, nw_ref, o_ref, st_ref, yt_ref):
    n_items = ATTN_SUBTILES * N_HEADS

    def scores_t(it):
        sub, hd = divmod(it, N_HEADS)
        g = hd // (N_HEADS // N_KV_HEADS)
        qp = q_ref[(hd // 2) * LANES:(hd // 2 + 1) * LANES, sub * TQ_ATTN:(sub + 1) * TQ_ATTN]
        st_ref[it % ATTN_SLOTS] = jnp.dot(kp_ref[2 * g + hd % 2], qp, preferred_element_type=F32)

    for it in range(ATTN_AHEAD):
        scores_t(it)
    for it in range(n_items):
        sub, hd = divmod(it, N_HEADS)
        g = hd // (N_HEADS // N_KV_HEADS)
        if it + ATTN_AHEAD < n_items:
            scores_t(it + ATTN_AHEAD)
        slot = it % ATTN_SLOTS
        m = jnp.max(st_ref[slot], axis=0, keepdims=True)
        p = jnp.exp2(st_ref[slot] - m)
        den = jnp.sum(p, axis=0, keepdims=True)
        num = jnp.dot(va_ref[g * HEAD_DIM:(g + 1) * HEAD_DIM, :], p.astype(BF16),
                      preferred_element_type=F32)
        yt_ref[hd * HEAD_DIM:(hd + 1) * HEAD_DIM, :] = num / den
        if hd == N_HEADS - 1:
            y = yt_ref[...].T
            ms = jnp.mean(y * y, axis=-1, keepdims=True)
            o_ref[sub * TQ_ATTN:(sub + 1) * TQ_ATTN, :] = (y * lax.rsqrt(ms + EPS) * nw_ref[...]).astype(BF16)


def _attention(q, kp, va, nw, batch, seq):
    tokens = batch * seq
    rows = ATTN_SUBTILES * TQ_ATTN
    nq = seq // rows
    return pl.pallas_call(
        _attn_kernel,
        grid=(batch, nq),
        in_specs=[pl.BlockSpec((None, D_ATTN, rows), lambda b, j: (b, 0, j)),
                  pl.BlockSpec((2 * N_KV_HEADS, seq, D_KV), lambda b, j: (0, b, 0)),
                  pl.BlockSpec((None, D_KV, seq), lambda b, j: (b, 0, 0)),
                  pl.BlockSpec((1, D_ATTN), lambda b, j: (0, 0))],
        out_specs=pl.BlockSpec((rows, D_ATTN), lambda b, j: (b * nq + j, 0)),
        out_shape=jax.ShapeDtypeStruct((tokens, D_ATTN), BF16),
        scratch_shapes=[pltpu.VMEM((ATTN_SLOTS, seq, TQ_ATTN), F32), pltpu.VMEM((D_ATTN, TQ_ATTN), F32)],
        compiler_params=_params(2),
        name="attn",
    )(q, kp, va, nw)


HG_LEVELS = (32, 16, 8)


def _hgrn_constants():
    n = HG_CHUNK
    idx = np.arange(n)
    fwd = (idx[None, :] <= idx[:, None]).astype(np.float32)
    bwd = fwd[::-1, ::-1]
    m2 = np.stack([np.tile(fwd, (1, 2)), np.tile(bwd, (1, 2))])

    lev = np.full((n, n), len(HG_LEVELS) + 1, np.int32)
    for t in range(n):
        for s in range(t + 1):
            if t // HG_FINE == s // HG_FINE:
                lev[t, s] = len(HG_LEVELS)
            else:
                for li, half in enumerate(HG_LEVELS):
                    if t // (2 * half) == s // (2 * half) and t // half != s // half:
                        lev[t, s] = li
                        break
    lev2 = np.stack([np.tile(lev, (1, HGRN_HEADS)), np.tile(lev[::-1, ::-1], (1, HGRN_HEADS))])
    head = np.arange(D_HGRN) // HEAD_DIM
    headmask = (head[:, None] == head[None, :]).astype(np.float32)
    return m2, lev2, headmask


def _hgrn_kernel(hq_ref, zf_ref, zb_ref, hi_ref, gate_ref, logit_ref, gnw_ref, m2_ref, lev_ref,
                 hmb_ref, seg_ref, o_ref, ofw_ref, obw_ref, st_ref, *, layer):
    seq = hq_ref.shape[0]
    n_chunks = seq // HG_CHUNK
    n = HG_CHUNK

    lg = [logit_ref[i] for i in range(DEPTH)]
    mx = functools.reduce(jnp.maximum, lg)
    ex = [jnp.exp(v - mx) for v in lg]
    den = functools.reduce(lambda a, b: a + b, ex)
    sm = [e / den for e in ex]
    acc = sm[0]
    for i in range(1, layer + 1):
        acc = acc + sm[i]
    lb = acc - sm[0]

    def stage_a(d, z_ref, lb_row, r0):
        z = z_ref[pl.ds(r0, n), :]
        f = lb_row + (1.0 - lb_row) * jax.nn.sigmoid(z)
        g = jnp.log(jnp.maximum(f, F_MIN)) * LOG2_E
        kk = 1.0 - f
        g1 = g.astype(BF16)
        g2 = (g - g1.astype(F32)).astype(BF16)
        gcat = jnp.concatenate([g1, g2], axis=0)
        e_cum = jnp.dot(m2_ref[d], gcat, preferred_element_type=F32)
        return kk, e_cum

    def level_exponent(e_cum, d, li):
        fine = li == len(HG_LEVELS)
        blk = HG_FINE if fine else HG_LEVELS[li]
        parts = []
        for r in range(0, n, blk):
            p_start = r if d == 0 else n - blk - r
            if fine:
                p_ref, later = p_start + HG_REF, True
            else:
                p_ref = (p_start // (2 * blk)) * 2 * blk + blk
                later = p_start >= p_ref
            t_ref = p_ref if d == 0 else n - 1 - p_ref
            ref = e_cum[t_ref:t_ref + 1]
            rows = e_cum[r:r + blk]
            parts.append(rows - ref if later else ref - rows)
        return jnp.concatenate(parts, axis=0)

    def stage_b(d, r0, kk, e_cum):
        hm_bf = hmb_ref[...]
        q = hq_ref[pl.ds(r0, n), :]
        v = hi_ref[pl.ds(r0, n), :]
        lev = lev_ref[d]
        a = jnp.zeros((n, D_HGRN), F32)
        for li in range(len(HG_LEVELS) + 1):
            e_lev = level_exponent(e_cum, d, li)
            x_q = jnp.exp2(e_lev)
            x_k = x_q if li < len(HG_LEVELS) else jnp.exp2(-e_lev)
            qt = (q * x_q).astype(BF16)
            kt = (kk * x_k).astype(BF16)
            kbd = jnp.concatenate([kt] * HGRN_HEADS, axis=0) * hm_bf
            sc = lax.dot_general(qt, kbd, (((1,), (1,)), ((), ())), preferred_element_type=F32)
            a = jnp.where(lev == li, sc, a)
        return a.astype(BF16)

    def stage_b2(d, r0, kk, e_cum, a):
        hm_bf = hmb_ref[...]
        q = hq_ref[pl.ds(r0, n), :]
        v = hi_ref[pl.ds(r0, n), :]
        vbd = jnp.concatenate([v.astype(BF16)] * HGRN_HEADS, axis=0) * hm_bf
        o_intra = jnp.dot(a, vbd, preferred_element_type=F32)

        e_tot = e_cum[n - 1:n] if d == 0 else e_cum[0:1]
        qb = (q * jnp.exp2(e_cum)).astype(BF16)
        kdec = kk * jnp.exp2(e_tot - e_cum)
        zpad = jnp.zeros((LANES - n - SUBLANES, D_HGRN), F32)
        tot8 = jnp.broadcast_to(e_tot, (SUBLANES, D_HGRN))
        kt = jnp.concatenate([kdec, tot8, zpad], axis=0).T
        decay_col = jnp.exp2(kt[:, n:n + 1])
        vpad = jnp.concatenate([v.astype(BF16), jnp.zeros((LANES - n, D_HGRN), BF16)], axis=0)
        upd = jnp.dot(kt.astype(BF16), vpad, preferred_element_type=F32)
        return o_intra, qb, upd, decay_col

    def stage_c(d, o_intra, qb, upd, decay_col):
        st = st_ref[d]
        o = o_intra + jnp.dot(qb, st.astype(BF16) * hmb_ref[...], preferred_element_type=F32)
        st_ref[d] = st * decay_col + upd
        return o

    st_ref[...] = jnp.zeros_like(st_ref)

    def scan_body(ci, carry):
        items = []
        for u in range(HG_UNROLL):
            cf = ci * HG_UNROLL + u
            items.append((0, zf_ref, lb[0:1], pl.multiple_of(cf * n, n), ofw_ref))
            items.append((1, zb_ref, lb[1:2], pl.multiple_of((n_chunks - 1 - cf) * n, n), obw_ref))
        sa = [stage_a(d, z_ref, lb_row, r0) for d, z_ref, lb_row, r0, _ in items]
        sb = [stage_b(d, r0, *a) for (d, _, _, r0, _), a in zip(items, sa)]
        sb2 = [stage_b2(d, r0, *a, b) for (d, _, _, r0, _), a, b in zip(items, sa, sb)]
        for (d, _, _, r0, out_ref), b in zip(items, sb2):
            out_ref[pl.ds(r0, n), :] = stage_c(d, *b)
        return carry

    lax.fori_loop(0, n_chunks // HG_UNROLL, scan_body, 0)

    gnw = gnw_ref[...]

    def finish_body(i, carry):
        r0 = pl.multiple_of(i * HG_FINISH_ROWS, HG_FINISH_ROWS)
        o = ofw_ref[pl.ds(r0, HG_FINISH_ROWS), :] + obw_ref[pl.ds(r0, HG_FINISH_ROWS), :]
        on = o * lax.rsqrt(_segment_mean_sq(o, seg_ref, HEAD_DIM) + EPS) * gnw
        gate = gate_ref[pl.ds(r0, HG_FINISH_ROWS), :]
        o_ref[pl.ds(r0, HG_FINISH_ROWS), :] = (on * (gate * jax.nn.sigmoid(gate))).astype(BF16)
        return carry

    lax.fori_loop(0, seq // HG_FINISH_ROWS, finish_body, 0, unroll=2)


def _hgrn(hg, logits, gnw, consts, seg, layer, batch, seq):
    tokens = hg.shape[0]
    m2, lev, hmb = consts
    col = lambda j: pl.BlockSpec((seq, D_HGRN), lambda b: (b, j))
    return pl.pallas_call(
        functools.partial(_hgrn_kernel, layer=layer),
        grid=(batch,),
        in_specs=[col(0), col(1), col(2), col(3), col(4),
                  _full(logits.shape), _full((1, D_HGRN)), _full(m2.shape), _full(lev.shape),
                  _full(hmb.shape), _full(seg.shape)],
        out_specs=pl.BlockSpec((seq, D_HGRN), lambda b: (b, 0)),
        out_shape=jax.ShapeDtypeStruct((tokens, D_HGRN), BF16),
        scratch_shapes=[pltpu.VMEM((seq, D_HGRN), F32), pltpu.VMEM((seq, D_HGRN), F32),
                        pltpu.VMEM((2, D_HGRN, D_HGRN), F32)],
        compiler_params=_params(1),
        name="hgrn",
    )(hg, hg, hg, hg, hg, logits, gnw, m2, lev, hmb, seg)


def _conv_kernel(a_ref, b_ref, dww_ref, dwb_ref, lnw_ref, lnb_ref, pww_ref, pwb_ref, nw_ref,
                 o_ref, pad_ref, dw_ref):
    seq = a_ref.shape[0]
    halo = jnp.zeros((CONV_HALO, D_CONV), F32)
    pad_ref[0:CONV_HALO, :] = halo
    pad_ref[CONV_HALO + seq:CONV_HALO + seq + CONV_HALO, :] = halo
    pad_ref[CONV_HALO:CONV_HALO + seq, :] = a_ref[...] * jax.nn.sigmoid(b_ref[...])

    win_rows = CONV_ROWS + 2 * CONV_HALO

    def tap_body(i, carry):
        r0 = pl.multiple_of(i * CONV_ROWS, CONV_ROWS)
        win = pad_ref[pl.ds(r0, win_rows), :]
        acc = jnp.zeros((CONV_ROWS, D_CONV), F32)
        for r in range(SUBLANES):
            shifted = win if r == 0 else pltpu.roll(win, win_rows - r, 0)
            for a8 in range(0, 2 * CONV_HALO, SUBLANES):
                tap = a8 + r - (CONV_HALO - CONV_PAD)
                if 0 <= tap < CONV_WIDTH:
                    acc = acc + shifted[a8:a8 + CONV_ROWS] * dww_ref[tap:tap + 1, :]
        dw_ref[pl.ds(r0, CONV_ROWS), :] = acc + dwb_ref[...]
        return carry

    lax.fori_loop(0, seq // CONV_ROWS, tap_body, 0, unroll=2)

    def post_body(i, carry):
        r0 = pl.multiple_of(i * CONV_POST_ROWS, CONV_POST_ROWS)
        u = dw_ref[pl.ds(r0, CONV_POST_ROWS), :]
        mu = jnp.mean(u, axis=-1, keepdims=True)
        uc = u - mu
        var = jnp.mean(uc * uc, axis=-1, keepdims=True)
        y = uc * lax.rsqrt(var + LN_EPS) * lnw_ref[...] + lnb_ref[...]
        y = y * jax.nn.sigmoid(y)
        y = jnp.dot(y.astype(BF16), pww_ref[...], preferred_element_type=F32) + pwb_ref[...]
        ms = jnp.mean(y * y, axis=-1, keepdims=True)
        o_ref[pl.ds(r0, CONV_POST_ROWS), :] = (y * lax.rsqrt(ms + EPS) * nw_ref[...]).astype(BF16)
        return carry

    lax.fori_loop(0, seq // CONV_POST_ROWS, post_body, 0, unroll=2)


def _conv(cv, dww, dwb, lnw, lnb, pww, pwb, nw, batch, seq):
    tokens = cv.shape[0]
    col = lambda j: pl.BlockSpec((seq, D_CONV), lambda b: (b, j))
    vec = _full((1, D_CONV))
    return pl.pallas_call(
        _conv_kernel,
        grid=(batch,),
        in_specs=[col(0), col(1), _full(dww.shape), vec, vec, vec, _full((D_CONV, D_CONV)), vec, vec],
        out_specs=pl.BlockSpec((seq, D_CONV), lambda b: (b, 0)),
        out_shape=jax.ShapeDtypeStruct((tokens, D_CONV), BF16),
        scratch_shapes=[pltpu.VMEM((seq + 2 * CONV_HALO, D_CONV), F32), pltpu.VMEM((seq, D_CONV), F32)],
        compiler_params=_params(1),
        name="conv",
    )(cv, cv, dww, dwb, lnw, lnb, pww, pwb, nw)


def _ffn_kernel(x_ref, ya_ref, yh_ref, yc_ref, wo_ref, nw_ref, wg_ref, wu_ref, wd_ref,
                o_ref, act_ref):
    x1 = x_ref[...]
    x1 = x1 + jnp.dot(ya_ref[...], wo_ref[0:D_ATTN, :], preferred_element_type=F32)
    x1 = x1 + jnp.dot(yh_ref[...], wo_ref[D_ATTN:D_ATTN + D_HGRN, :], preferred_element_type=F32)
    x1 = x1 + jnp.dot(yc_ref[...], wo_ref[D_ATTN + D_HGRN:D_MODEL, :], preferred_element_type=F32)
    ms = jnp.mean(x1 * x1, axis=-1, keepdims=True)
    h = (x1 * lax.rsqrt(ms + EPS) * nw_ref[...]).astype(BF16)
    for c in range(0, D_FF, FF_CHUNK):
        g = jnp.dot(h, wg_ref[:, c:c + FF_CHUNK], preferred_element_type=F32)
        u = jnp.dot(h, wu_ref[:, c:c + FF_CHUNK], preferred_element_type=F32)
        act_ref[:, c:c + FF_CHUNK] = (g * jax.nn.sigmoid(g) * u).astype(BF16)
    o_ref[...] = x1 + jnp.dot(act_ref[...], wd_ref[...], preferred_element_type=F32)


def _ffn(x2, ya, yh, yc, wo, nw, wg, wu, wd, layer):
    tokens = x2.shape[0]
    row = lambda w: pl.BlockSpec((TM_FFN, w), lambda i: (i, 0))
    resident = lambda shape: _layer_block(shape, layer, pipeline_mode=pl.Buffered(1))
    return pl.pallas_call(
        _ffn_kernel,
        grid=(tokens // TM_FFN,),
        in_specs=[row(D_MODEL), row(D_ATTN), row(D_HGRN), row(D_CONV),
                  resident((D_MODEL, D_MODEL)), _full((1, D_MODEL)),
                  resident((D_MODEL, D_FF)), resident((D_MODEL, D_FF)), resident((D_FF, D_MODEL))],
        out_specs=row(D_MODEL),
        out_shape=jax.ShapeDtypeStruct((tokens, D_MODEL), F32),
        scratch_shapes=[pltpu.VMEM((TM_FFN, D_FF), BF16)],
        compiler_params=_params(1),
        name="ffn",
    )(x2, ya, yh, yc, wo, nw, wg, wu, wd)


def _rope_tables(seq):
    rows = seq // GRID_W
    row_id = jnp.repeat(jnp.arange(rows, dtype=F32), GRID_W)
    col_id = jnp.tile(jnp.arange(GRID_W, dtype=F32), rows)
    half = HEAD_DIM // 2
    inv_freq = ROPE_THETA ** (-jnp.arange(0, half, 2, dtype=F32) / half)
    ang_r = row_id[:, None] * inv_freq[None, :]
    ang_c = col_id[:, None] * inv_freq[None, :]
    ang = jnp.concatenate([ang_r, ang_r, ang_c, ang_c], axis=-1)
    cos, sin = jnp.cos(ang), jnp.sin(ang)
    lane = np.arange(HEAD_DIM)
    sign = jnp.asarray(np.where((lane % (HEAD_DIM // 2)) < HEAD_DIM // 4, -1.0, 1.0), F32)
    sin_signed = sin * sign[None, :]
    scale = HEAD_DIM ** -0.5 * np.log2(np.e)
    return ((jnp.tile(cos, (1, N_HEADS)) * scale).T, (jnp.tile(sin_signed, (1, N_HEADS)) * scale).T,
            jnp.tile(cos, (1, N_KV_HEADS)), jnp.tile(sin_signed, (1, N_KV_HEADS)))


def _segment_matrix(width, seg_len):
    seg = np.arange(width) // seg_len
    return jnp.asarray((seg[:, None] == seg[None, :]).astype(np.float32), BF16)


def kernel(x, mix_norm_w, w_in, q_norm_w, k_norm_w, hgrn_lb_logits, hgrn_gnorm_w, conv_dw_w, conv_dw_b,
           conv_ln_w, conv_ln_b, conv_pw_w, conv_pw_b, attn_out_norm_w, conv_out_norm_w, w_out,
           ffn_norm_w, w_gate, w_up, w_down):
    batch, seq, d_model = x.shape
    assert d_model == D_MODEL and seq % TM_PROJ == 0 and seq % HG_CHUNK == 0
    assert seq % (ATTN_SUBTILES * TQ_ATTN) == 0 and seq % CONV_POST_ROWS == 0 and seq % HG_FINISH_ROWS == 0
    assert (batch * seq) % TM_FFN == 0 and w_in.shape == (DEPTH, D_MODEL, D_IN_PROJ)
    assert w_gate.shape == (DEPTH, D_MODEL, D_FF)

    tabs = _rope_tables(seq)
    segq = _segment_matrix(D_ATTN, HEAD_DIM)
    segk = _segment_matrix(D_KV, HEAD_DIM)
    segh = _segment_matrix(D_HGRN, HEAD_DIM)
    m2, lev, hm = _hgrn_constants()
    hconsts = (jnp.asarray(m2, BF16), jnp.asarray(lev), jnp.asarray(hm, BF16))
    logits = hgrn_lb_logits.astype(F32)

    w_in_b, w_out_b = w_in.astype(BF16), w_out.astype(BF16)
    w_gate_b, w_up_b, w_down_b = w_gate.astype(BF16), w_up.astype(BF16), w_down.astype(BF16)

    x2 = x.reshape(batch * seq, D_MODEL)
    for l in range(DEPTH):
        q, kp, va, hg, cv = _inproj(
            x2, mix_norm_w[l][None, :], w_in_b,
            jnp.tile(q_norm_w[l], N_HEADS)[:, None], jnp.tile(k_norm_w[l], N_KV_HEADS)[None, :],
            tabs, segq, segk, l, batch, seq)
        ya = _attention(q, kp, va, attn_out_norm_w[l][None, :], batch, seq)
        yh = _hgrn(hg, logits, jnp.tile(hgrn_gnorm_w[l], HGRN_HEADS)[None, :], hconsts, segh, l, batch, seq)
        yc = _conv(cv, conv_dw_w[l], conv_dw_b[l][None, :], conv_ln_w[l][None, :], conv_ln_b[l][None, :],
                   conv_pw_w[l].astype(BF16), conv_pw_b[l][None, :], conv_out_norm_w[l][None, :], batch, seq)
        x2 = _ffn(x2, ya, yh, yc, w_out_b, ffn_norm_w[l][None, :], w_gate_b, w_up_b, w_down_b, l)
    return x2.reshape(batch, seq, D_MODEL)
```
